```python
import jax, jax.numpy as jnp
from jax import lax
import numpy as np

D_MODEL = 1024
BATCH = 4
SEQ = 8192
DEPTH = 1

HEAD_DIM = 64
N_ATT_HEADS = (D_MODEL // 2) // HEAD_DIM
N_RWKV_HEADS = (D_MODEL // 2) // HEAD_DIM
ATT_W = N_ATT_HEADS * HEAD_DIM
RWKV_W = N_RWKV_HEADS * HEAD_DIM
MIX_W = ATT_W + RWKV_W
GRID_W = 64
WIN_H = 8
WIN_W = 16
ROWS_PER_BLOCK = 2
D_DECAY_LORA = 64
D_AAA_LORA = 64
D_GATE_LORA = 128
N_DIR = 2
SHIFT_W = 3 * RWKV_W + N_DIR * D_DECAY_LORA + N_DIR * D_AAA_LORA + D_GATE_LORA
IN_PROJ_W = 3 * ATT_W + SHIFT_W
N_EXPERTS = 32
TOP_K = 4
D_EXPERT = D_MODEL
SWIGLU_ALPHA = 1.702
SWIGLU_LIMIT = 7.0
EXPERT_BLOCK = 128
RMS_EPS = 1e-5
GN_EPS = 64e-5

kernel_name = "hybrid_na2d_birwkv7_moe_block"


def rmsnorm(x, w):
    xf = x.astype(jnp.float32)
    y = xf * lax.rsqrt(jnp.mean(xf * xf, axis=-1, keepdims=True) + RMS_EPS)
    return (y * w.astype(jnp.float32)).astype(x.dtype)


def neighbourhood_attention(q, k, v, rpb):
    B, T, H, dh = q.shape
    rows = T // GRID_W
    kh = min(WIN_H, rows)
    kw = WIN_W
    qg = q.reshape(B, rows, GRID_W, H, dh)
    kg = k.reshape(B, rows, GRID_W, H, dh)
    vg = v.reshape(B, rows, GRID_W, H, dh)
    cols = np.arange(GRID_W)
    col_start = np.clip(cols - kw // 2, 0, GRID_W - kw)
    col_idx = col_start[:, None] + np.arange(kw)[None, :]
    dc = col_idx - cols[:, None] + (WIN_W - 1)
    n_blocks = rows // ROWS_PER_BLOCK
    scale = dh ** -0.5

    def block(bi):
        r = bi * ROWS_PER_BLOCK + jnp.arange(ROWS_PER_BLOCK)
        r_start = jnp.clip(r - kh // 2, 0, rows - kh)
        row_idx = r_start[:, None] + jnp.arange(kh)[None, :]
        q_blk = lax.dynamic_slice_in_dim(qg, bi * ROWS_PER_BLOCK, ROWS_PER_BLOCK, axis=1)
        k_win = kg[:, row_idx][:, :, :, col_idx]
        v_win = vg[:, row_idx][:, :, :, col_idx]
        s = jnp.einsum('brchd,bricjhd->bhrcij', q_blk, k_win).astype(jnp.float32) * scale
        dr = row_idx - r[:, None] + (WIN_H - 1)
        bias = rpb[:, dr[:, None, :, None], dc[None, :, None, :]]
        s = s + bias.astype(jnp.float32)[None]
        p = jax.nn.softmax(s.reshape(s.shape[:4] + (kh * kw,)), axis=-1).reshape(s.shape)
        return jnp.einsum('bhrcij,bricjhd->brchd', p.astype(v.dtype), v_win)

    out = lax.map(block, jnp.arange(n_blocks))
    return jnp.moveaxis(out, 0, 1).reshape(B, T, H, dh)


def rwkv7_bidir(p, mu_prev, mu_next, w0, w2, a0, a2, g2, k_k, k_a, r_k, ln_w, ln_b):
    B, T, _ = p.shape
    H, N, C = N_RWKV_HEADS, HEAD_DIM, RWKV_W
    p_prev = jnp.pad(p, ((0, 0), (1, 0), (0, 0)))[:, :-1]
    p_next = jnp.pad(p, ((0, 0), (0, 1), (0, 0)))[:, 1:]
    p = p + mu_prev * (p_prev - p) + mu_next * (p_next - p)
    split_at = list(np.cumsum([C, C, C, N_DIR * D_DECAY_LORA, N_DIR * D_AAA_LORA]))
    r, k, v, dw, da, dg = jnp.split(p, split_at, axis=-1)
    dw = dw.reshape(B, T, N_DIR, D_DECAY_LORA)
    da = da.reshape(B, T, N_DIR, D_AAA_LORA)
    w_log = -jax.nn.softplus(-(w0[:, None, None, :] + jnp.einsum('btzr,zrc->zbtc', jnp.tanh(dw), w2))) - 0.5
    decay = jnp.exp(-jnp.exp(w_log.astype(jnp.float32)))
    a = jax.nn.sigmoid(a0[:, None, None, :] + jnp.einsum('btzr,zrc->zbtc', da, a2))
    g = jax.nn.sigmoid(dg) @ g2
    heads = lambda u: u.reshape(u.shape[:-1] + (H, N))
    kk = heads((k * k_k).astype(jnp.float32))
    kk = kk / jnp.maximum(jnp.sqrt(jnp.sum(kk * kk, axis=-1, keepdims=True)), 1e-12)
    kk_c = kk.reshape(B, T, C)
    k_dir = k[None] * (1.0 + (a - 1.0) * k_a)
    a_kk = a.astype(jnp.float32) * kk_c[None]
    both = lambda u: jnp.broadcast_to(u[None], (N_DIR,) + u.shape)

    def orient(u):
        return jnp.stack([u[0], jnp.flip(u[1], axis=1)])

    to_scan = lambda u: jnp.moveaxis(orient(heads(u.astype(jnp.float32))), 2, 0)
    xs = (to_scan(both(r)), to_scan(decay), to_scan(k_dir), to_scan(both(v)),
          to_scan(both(kk_c)), to_scan(a_kk))

    def step(S, inp):
        r_t, w_t, k_t, v_t, kk_t, akk_t = inp
        sa = jnp.einsum('zbhij,zbhj->zbhi', S, kk_t)
        S = S * w_t[..., None, :] - sa[..., :, None] * akk_t[..., None, :] + v_t[..., :, None] * k_t[..., None, :]
        return S, jnp.einsum('zbhij,zbhj->zbhi', S, r_t)

    S0 = jnp.zeros((N_DIR, B, H, N, N), jnp.float32)
    _, ys = lax.scan(step, S0, xs)
    y = orient(jnp.moveaxis(ys, 0, 2)).sum(axis=0)
    mean = jnp.mean(y, axis=-1, keepdims=True)
    var = jnp.mean((y - mean) ** 2, axis=-1, keepdims=True)
    yn = ((y - mean) * lax.rsqrt(var + GN_EPS)).reshape(B, T, C) * ln_w.astype(jnp.float32) + ln_b.astype(jnp.float32)
    coef = jnp.einsum('bthn,zbthn,hn->bth', heads(r.astype(jnp.float32)), heads(k_dir.astype(jnp.float32)),
                      r_k.astype(jnp.float32))
    bonus = (coef[..., None] * heads(v.astype(jnp.float32))).reshape(B, T, C)
    return ((yn + bonus) * g.astype(jnp.float32)).astype(p.dtype)


def clamped_swiglu(hcat):
    gate, lin = hcat[:, :D_EXPERT], hcat[:, D_EXPERT:]
    gate = jnp.minimum(gate, SWIGLU_LIMIT)
    lin = jnp.clip(lin, -SWIGLU_LIMIT, SWIGLU_LIMIT)
    return (lin + 1.0) * (gate * jax.nn.sigmoid(SWIGLU_ALPHA * gate))


def moe(x2d, router_w, router_b, w_up, b_up, w_down, b_down):
    M, D = x2d.shape
    logits = (x2d @ router_w + router_b).astype(jnp.float32)
    top_val, top_idx = lax.top_k(logits, TOP_K)
    gates = jax.nn.softmax(top_val, axis=-1)
    A = M * TOP_K
    e_flat = top_idx.reshape(A)
    tok_flat = jnp.arange(A, dtype=jnp.int32) // TOP_K
    g_flat = gates.reshape(A)
    order = jnp.argsort(e_flat)
    e_s, tok_s, g_s = e_flat[order], tok_flat[order], g_flat[order]
    counts = jnp.bincount(e_flat, length=N_EXPERTS)
    padded = (counts + EXPERT_BLOCK - 1) // EXPERT_BLOCK * EXPERT_BLOCK
    start = jnp.cumsum(counts) - counts
    pend = jnp.cumsum(padded)
    pstart = pend - padded
    dest = pstart[e_s] + (jnp.arange(A) - start[e_s])
    P = -(-(A + N_EXPERTS * EXPERT_BLOCK) // EXPERT_BLOCK) * EXPERT_BLOCK
    n_blk = P // EXPERT_BLOCK
    row_tok = jnp.zeros((P,), jnp.int32).at[dest].set(tok_s)
    x_buf = x2d[row_tok].reshape(n_blk, EXPERT_BLOCK, D)
    blk_expert = jnp.searchsorted(pend, jnp.arange(n_blk) * EXPERT_BLOCK, side='right')
    blk_expert = jnp.minimum(blk_expert, N_EXPERTS - 1)

    def expert_block(args):
        xb, e = args
        hcat = xb @ w_up[e] + b_up[e]
        return clamped_swiglu(hcat) @ w_down[e] + b_down[e]

    y_buf = lax.map(expert_block, (x_buf, blk_expert)).reshape(P, D)
    contrib = g_s[:, None].astype(x2d.dtype) * y_buf[dest]
    return jax.ops.segment_sum(contrib, tok_s, num_segments=M)


def setup_inputs(seed: int = 0) -> dict:
    key = jax.random.key(seed)
    ks = jax.random.split(key, 32)
    L, D, C = DEPTH, D_MODEL, RWKV_W
    nrm = lambda k, s, sc: jax.random.normal(k, s, jnp.float32) * sc
    return {
        "x": nrm(ks[0], (BATCH, SEQ, D), 1.0),
        "norm1_w": 1.0 + nrm(ks[1], (L, D), 0.05),
        "w_in": nrm(ks[2], (L, D, IN_PROJ_W), D ** -0.5),
        "attn_rpb": nrm(ks[3], (L, N_ATT_HEADS, 2 * WIN_H - 1, 2 * WIN_W - 1), 0.1),
        "attn_norm_w": 1.0 + nrm(ks[4], (L, ATT_W), 0.05),
        "rwkv_mu_prev": jax.random.uniform(ks[5], (L, SHIFT_W), jnp.float32, 0.0, 0.5),
        "rwkv_mu_next": jax.random.uniform(ks[6], (L, SHIFT_W), jnp.float32, 0.0, 0.5),
        "rwkv_w0": nrm(ks[7], (L, N_DIR, C), 0.5),
        "rwkv_w2": nrm(ks[8], (L, N_DIR, D_DECAY_LORA, C), 0.1),
        "rwkv_a0": nrm(ks[9], (L, N_DIR, C), 0.5),
        "rwkv_a2": nrm(ks[10], (L, N_DIR, D_AAA_LORA, C), 0.1),
        "rwkv_g2": nrm(ks[11], (L, D_GATE_LORA, C), D_GATE_LORA ** -0.5),
        "rwkv_k_k": 0.85 + nrm(ks[12], (L, C), 0.05),
        "rwkv_k_a": 1.0 + nrm(ks[13], (L, C), 0.05),
        "rwkv_r_k": nrm(ks[14], (L, N_RWKV_HEADS, HEAD_DIM), 0.1),
        "rwkv_ln_w": 1.0 + nrm(ks[15], (L, C), 0.05),
        "rwkv_ln_b": nrm(ks[16], (L, C), 0.01),
        "w_o": nrm(ks[17], (L, MIX_W, D), MIX_W ** -0.5),
        "norm2_w": 1.0 + nrm(ks[18], (L, D), 0.05),
        "router_w": nrm(ks[19], (L, D, N_EXPERTS), D ** -0.5),
        "router_b": nrm(ks[20], (L, N_EXPERTS), 0.01),
        "expert_w_up": nrm(ks[21], (L, N_EXPERTS, D, 2 * D_EXPERT), D ** -0.5),
        "expert_b_up": nrm(ks[22], (L, N_EXPERTS, 2 * D_EXPERT), 0.01),
        "expert_w_down": nrm(ks[23], (L, N_EXPERTS, D_EXPERT, D), D_EXPERT ** -0.5),
        "expert_b_down": nrm(ks[24], (L, N_EXPERTS, D), 0.01),
        "final_norm_w": 1.0 + nrm(ks[25], (D,), 0.05),
    }


def reference(x, norm1_w, w_in, attn_rpb, attn_norm_w, rwkv_mu_prev, rwkv_mu_next, rwkv_w0, rwkv_w2,
              rwkv_a0, rwkv_a2, rwkv_g2, rwkv_k_k, rwkv_k_a, rwkv_r_k, rwkv_ln_w, rwkv_ln_b, w_o,
              norm2_w, router_w, router_b, expert_w_up, expert_b_up, expert_w_down, expert_b_down,
              final_norm_w):
    B, T, D = x.shape
    h = x
    for l in range(DEPTH):
        hn = rmsnorm(h, norm1_w[l])
        proj = hn @ w_in[l]
        q = proj[..., :ATT_W].reshape(B, T, N_ATT_HEADS, HEAD_DIM)
        k = proj[..., ATT_W:2 * ATT_W].reshape(B, T, N_ATT_HEADS, HEAD_DIM)
        v = proj[..., 2 * ATT_W:3 * ATT_W].reshape(B, T, N_ATT_HEADS, HEAD_DIM)
        att = neighbourhood_attention(q, k, v, attn_rpb[l]).reshape(B, T, ATT_W)
        att = rmsnorm(att, attn_norm_w[l])
        rk = rwkv7_bidir(proj[..., 3 * ATT_W:], rwkv_mu_prev[l], rwkv_mu_next[l], rwkv_w0[l], rwkv_w2[l],
                         rwkv_a0[l], rwkv_a2[l], rwkv_g2[l], rwkv_k_k[l], rwkv_k_a[l], rwkv_r_k[l],
                         rwkv_ln_w[l], rwkv_ln_b[l])
        h = h + jnp.concatenate([att, rk], axis=-1) @ w_o[l]
        hn = rmsnorm(h, norm2_w[l])
        h = h + moe(hn.reshape(B * T, D), router_w[l], router_b[l], expert_w_up[l], expert_b_up[l],
                    expert_w_down[l], expert_b_down[l]).reshape(B, T, D)
    return rmsnorm(h, final_norm_w)
```

```python
import functools

import numpy as np
import jax
import jax.numpy as jnp
from jax import lax
from jax.experimental import pallas as pl
from jax.experimental.pallas import tpu as pltpu

F32 = jnp.float32
BF16 = jnp.bfloat16

HEAD_DIM = 64
GRID_W = 64
WIN_H = 8
WIN_W = 16
N_EXPERTS = 32
TOP_K = 4
SWIGLU_ALPHA = 1.702
SWIGLU_LIMIT = 7.0
RMS_EPS = 1e-5
GN_EPS = 64e-5
D_LORA = 64
D_GATE_LORA = 128
CHUNK = 64
LANES = 128
MASK_NEG = -1e30
VMEM_LIMIT = 56 * 1024 * 1024


def _cparams(sem):
    return pltpu.CompilerParams(dimension_semantics=sem, vmem_limit_bytes=VMEM_LIMIT)


def _dot(a, b):
    return jnp.dot(a, b, preferred_element_type=F32)


def _split3(x):
    hi = x.astype(BF16)
    r1 = x - hi.astype(F32)
    mid = r1.astype(BF16)
    lo = (r1 - mid.astype(F32)).astype(BF16)
    return hi, mid, lo


def _dot_exact_lhs(a01, x):
    hi, mid, lo = _split3(x)
    return _dot(a01, hi) + _dot(a01, mid) + _dot(a01, lo)


def _dot_exact_rhs(x, b01):
    hi, mid, lo = _split3(x)
    return _dot(hi, b01) + _dot(mid, b01) + _dot(lo, b01)


def _inproj_kernel(x_ref, nw_ref, w_ref, qkv_ref, p_ref):
    x = x_ref[...]
    ms = jnp.mean(x * x, axis=-1, keepdims=True)
    hn = (x * lax.rsqrt(ms + RMS_EPS) * nw_ref[...]).astype(BF16)
    n_qkv = qkv_ref.shape[-1]
    qkv_ref[...] = _dot(hn, w_ref[:, :n_qkv]).astype(BF16)
    p_ref[...] = _dot(hn, w_ref[:, n_qkv:]).astype(BF16)


def _in_proj(x2d, norm_w, w_in_bf16, n_qkv, tm=512):
    M, D = x2d.shape
    n_all = w_in_bf16.shape[1]
    n_p = n_all - n_qkv
    return pl.pallas_call(
        _inproj_kernel,
        grid=(M // tm,),
        in_specs=[
            pl.BlockSpec((tm, D), lambda i: (i, 0)),
            pl.BlockSpec((1, D), lambda i: (0, 0)),
            pl.BlockSpec((D, n_all), lambda i: (0, 0)),
        ],
        out_specs=[
            pl.BlockSpec((tm, n_qkv), lambda i: (i, 0)),
            pl.BlockSpec((tm, n_p), lambda i: (i, 0)),
        ],
        out_shape=[jax.ShapeDtypeStruct((M, n_qkv), BF16), jax.ShapeDtypeStruct((M, n_p), BF16)],
        compiler_params=_cparams(("parallel",)),
        name="in_proj",
    )(x2d, norm_w.reshape(1, D), w_in_bf16)


def _na_bias_table(rpb):
    H = rpb.shape[0]
    cols = np.arange(GRID_W)
    col_start = np.clip(cols - WIN_W // 2, 0, GRID_W - WIN_W)
    kc = np.arange(GRID_W)
    valid = (kc[None, :] >= col_start[:, None]) & (kc[None, :] < col_start[:, None] + WIN_W)
    dc = np.clip(kc[None, :] - cols[:, None] + (WIN_W - 1), 0, 2 * WIN_W - 2)
    var = np.arange(WIN_H)
    wi = np.arange(WIN_H)
    dr = wi[None, :] - var[:, None] + (WIN_H - 1)
    tab = rpb[:, dr[:, :, None, None], dc[None, None, :, :]]
    tab = jnp.where(valid[None, None, None], tab.astype(F32), MASK_NEG)
    tab = jnp.transpose(tab, (1, 0, 3, 2, 4))
    return tab.reshape(WIN_H, H, GRID_W, WIN_H * GRID_W)


def _na_kernel(q_ref, k_ref, v_ref, bias_ref, o_ref, *, rows_per_step, n_rows):
    i = pl.program_id(2)
    lane = lax.broadcasted_iota(jnp.int32, (1, LANES), 1)
    head0 = lane < HEAD_DIM
    scale = HEAD_DIM ** -0.5
    nkeys = WIN_H * GRID_W
    for rr in range(rows_per_step):
        r = i * rows_per_step + rr
        r_start = jnp.clip(r - WIN_H // 2, 0, n_rows - WIN_H)
        var = r - r_start
        q = q_ref[0, rr * GRID_W:(rr + 1) * GRID_W, :]
        koff = pl.multiple_of(r_start * GRID_W, GRID_W)
        kwin = k_ref[0, pl.ds(koff, nkeys), :]
        vwin = v_ref[0, pl.ds(koff, nkeys), :]
        outs = []
        for h in range(2):
            qh = jnp.where(head0 if h == 0 else jnp.logical_not(head0), q, jnp.zeros_like(q))
            s = lax.dot_general(qh, kwin, (((1,), (1,)), ((), ())), preferred_element_type=F32)
            s = s * scale + bias_ref[var, h]
            m = jnp.max(s, axis=-1, keepdims=True)
            e = jnp.exp(s - m)
            denom = jnp.sum(e, axis=-1, keepdims=True)
            o = _dot(e.astype(BF16), vwin) / denom
            outs.append(o)
        o_ref[0, rr * GRID_W:(rr + 1) * GRID_W, :] = jnp.where(head0, outs[0], outs[1]).astype(o_ref.dtype)


def _na2d(qkv, bias_tab, att_w, rows_per_step=8):
    B, T, _ = qkv.shape
    n_rows = T // GRID_W
    assert n_rows >= WIN_H and n_rows % rows_per_step == 0
    n_pairs = att_w // LANES
    tq = rows_per_step * GRID_W
    kernel = functools.partial(_na_kernel, rows_per_step=rows_per_step, n_rows=n_rows)
    return pl.pallas_call(
        kernel,
        grid=(B, n_pairs, n_rows // rows_per_step),
        in_specs=[
            pl.BlockSpec((1, tq, LANES), lambda b, hp, i: (b, i, hp)),
            pl.BlockSpec((1, T, LANES), lambda b, hp, i: (b, 0, n_pairs + hp)),
            pl.BlockSpec((1, T, LANES), lambda b, hp, i: (b, 0, 2 * n_pairs + hp)),
            pl.BlockSpec((WIN_H, 2, GRID_W, WIN_H * GRID_W), lambda b, hp, i: (0, hp, 0, 0)),
        ],
        out_specs=pl.BlockSpec((1, tq, LANES), lambda b, hp, i: (b, i, hp)),
        out_shape=jax.ShapeDtypeStruct((B, T, att_w), BF16),
        compiler_params=_cparams(("parallel", "parallel", "arbitrary")),
        name="na2d",
    )(qkv, qkv, qkv, bias_tab)


def _prep_kernel(p_ref, pprev_ref, pnext_ref, mup_ref, mun_ref, w0_ref, w2_ref, a0_ref, a2_ref, g2_ref,
                 kk_ref, ka_ref, rk_ref, bd_ref, tri_ref, ones_ref,
                 kq_ref, rq_ref, kd_ref, bk_ref, egl_ref, v_ref, bonus_ref, gate_ref, *, tb, c):
    i = pl.program_id(1)
    nb = pl.num_programs(1)
    p = p_ref[0].astype(F32)
    halo = pprev_ref.shape[1]
    prow = jnp.where(i > 0, pprev_ref[0, halo - 1:halo, :].astype(F32), 0.0)
    nrow = jnp.where(i < nb - 1, pnext_ref[0, 0:1, :].astype(F32), 0.0)
    ridx = lax.broadcasted_iota(jnp.int32, (tb, 1), 0)
    p_prev = jnp.where(ridx == 0, prow, pltpu.roll(p, 1, 0))
    p_next = jnp.where(ridx == tb - 1, nrow, pltpu.roll(p, tb - 1, 0))
    pm = p + mup_ref[...] * (p_prev - p) + mun_ref[...] * (p_next - p)
    r = pm[:, :c]
    k = pm[:, c:2 * c]
    v = pm[:, 2 * c:3 * c]
    o = 3 * c
    dw = pm[:, o:o + 2 * D_LORA]
    da = pm[:, o + 2 * D_LORA:o + 4 * D_LORA]
    dg = pm[:, o + 4 * D_LORA:o + 4 * D_LORA + D_GATE_LORA]

    wl = w0_ref[...] + _dot(jnp.tanh(dw).astype(BF16), w2_ref[...])
    neg = -wl
    softplus = jnp.maximum(neg, 0.0) + jnp.log(1.0 + jnp.exp(-jnp.abs(neg)))
    logw = -jnp.exp(-softplus - 0.5)
    a = jax.nn.sigmoid(a0_ref[...] + _dot(da.astype(BF16), a2_ref[...]))
    gate = _dot(jax.nn.sigmoid(dg).astype(BF16), g2_ref[...])

    bd = bd_ref[...]
    kkr = k * kk_ref[...]
    ssq = _dot_exact_rhs(kkr * kkr, bd)
    kk = kkr / jnp.maximum(jnp.sqrt(ssq), 1e-12)

    ksum = jnp.zeros_like(k)
    nchunk = tb // CHUNK
    for z in range(2):
        a_z = a[:, z * c:(z + 1) * c]
        lw = logw[:, z * c:(z + 1) * c]
        kdir = k * (1.0 + (a_z - 1.0) * ka_ref[...])
        ksum = ksum + kdir
        b_z = a_z * kk
        g_incl = _dot_exact_lhs(tri_ref[z], lw)
        g_all = _dot_exact_lhs(ones_ref[...], lw)
        e_neg = jnp.exp(-g_incl)
        kq_ref[z, 0] = (kk * jnp.exp(g_incl - lw)).astype(BF16)
        rq_ref[z, 0] = (r * jnp.exp(g_incl)).astype(BF16)
        kd_ref[z, 0] = (kdir * e_neg).astype(BF16)
        bk_ref[z, 0] = (b_z * e_neg).astype(BF16)
        egl_ref[z, 0] = jnp.exp(g_all).reshape(nchunk, CHUNK, c)[:, 0:1, :]
    coef = _dot_exact_rhs(r * ksum * rk_ref[...], bd)
    v_ref[0] = v.astype(BF16)
    bonus_ref[0] = (coef * v).astype(BF16)
    gate_ref[0] = gate.astype(BF16)


def _rwkv_prep(p, mu_prev, mu_next, w0, w2, a0, a2, g2, k_k, k_a, r_k, tb=256):
    B, T, pw = p.shape
    c = k_k.shape[0]
    assert T % tb == 0 and tb % CHUNK == 0
    halo = 16
    nhb = T // halo
    nchunk = tb // CHUNK
    w2bd = jnp.zeros((2 * D_LORA, 2 * c), F32)
    a2bd = jnp.zeros((2 * D_LORA, 2 * c), F32)
    for z in range(2):
        w2bd = w2bd.at[z * D_LORA:(z + 1) * D_LORA, z * c:(z + 1) * c].set(w2[z])
        a2bd = a2bd.at[z * D_LORA:(z + 1) * D_LORA, z * c:(z + 1) * c].set(a2[z])
    ch = np.arange(c) // HEAD_DIM
    bd = jnp.asarray((ch[:, None] == ch[None, :]).astype(np.float32), BF16)
    t = np.arange(tb)
    same = (t[:, None] // CHUNK) == (t[None, :] // CHUNK)
    tri = np.stack([same & (t[None, :] <= t[:, None]), same & (t[None, :] >= t[:, None])]).astype(np.float32)
    tri = jnp.asarray(tri, BF16)
    ones = jnp.asarray(same.astype(np.float32), BF16)
    row = lambda u: u.reshape(1, -1).astype(F32)
    const = lambda shape: pl.BlockSpec(shape, lambda b, i: (0,) * len(shape))
    dir_out = pl.BlockSpec((2, 1, tb, c), lambda b, i: (0, b, i, 0))
    tok_out = pl.BlockSpec((1, tb, c), lambda b, i: (b, i, 0))
    kernel = functools.partial(_prep_kernel, tb=tb, c=c)
    return pl.pallas_call(
        kernel,
        grid=(B, T // tb),
        in_specs=[
            pl.BlockSpec((1, tb, pw), lambda b, i: (b, i, 0)),
            pl.BlockSpec((1, halo, pw), lambda b, i: (b, jnp.maximum(i * (tb // halo) - 1, 0), 0)),
            pl.BlockSpec((1, halo, pw), lambda b, i: (b, jnp.minimum((i + 1) * (tb // halo), nhb - 1), 0)),
            const((1, pw)), const((1, pw)),
            const((1, 2 * c)), const((2 * D_LORA, 2 * c)),
            const((1, 2 * c)), const((2 * D_LORA, 2 * c)),
            const((D_GATE_LORA, c)),
            const((1, c)), const((1, c)), const((1, c)),
            const((c, c)), const((2, tb, tb)), const((tb, tb)),
        ],
        out_specs=[dir_out, dir_out, dir_out, dir_out,
                   pl.BlockSpec((2, 1, nchunk, 1, c), lambda b, i: (0, b, i, 0, 0)),
                   tok_out, tok_out, tok_out],
        out_shape=[jax.ShapeDtypeStruct((2, B, T, c), BF16)] * 4
        + [jax.ShapeDtypeStruct((2, B, T // CHUNK, 1, c), F32)]
        + [jax.ShapeDtypeStruct((B, T, c), BF16)] * 3,
        compiler_params=_cparams(("parallel", "parallel")),
        name="rwkv_prep",
    )(p, p, p, row(mu_prev), row(mu_next), row(w0), w2bd.astype(BF16), row(a0), a2bd.astype(BF16),
      g2.astype(BF16), row(k_k), row(k_a), row(r_k), bd, tri, ones)


def _bmm(a, b):
    return jnp.einsum("cij,cjk->cik", a.astype(BF16), b.astype(BF16), preferred_element_type=F32)


def _bmm_nt(a, b):
    return jnp.einsum("cik,cjk->cij", a.astype(BF16), b.astype(BF16), preferred_element_type=F32)


def _scan_kernel(kq_ref, rq_ref, kd_ref, bk_ref, v_ref, egl_ref, y_ref, h_s, mc_s, cc_s, hs_s, *, cg):
    z = pl.program_id(0)
    gi = pl.program_id(3)

    @pl.when(gi == 0)
    def _():
        h_s[...] = jnp.zeros_like(h_s)

    two = 2 * CHUNK
    lane = lax.broadcasted_iota(jnp.int32, (1, 1, LANES), 2)
    head0 = lane < HEAD_DIM

    def stacked(x):
        x = x.astype(F32).reshape(cg, CHUNK, LANES)
        return jnp.concatenate([jnp.where(head0, x, 0.0), jnp.where(head0, 0.0, x)], axis=1)

    kq = stacked(kq_ref[0, 0])
    rq = stacked(rq_ref[0, 0])
    kd = stacked(kd_ref[0, 0])
    bk = stacked(bk_ref[0, 0])
    v = stacked(v_ref[0])
    egl = egl_ref[0, 0]

    row = lax.broadcasted_iota(jnp.int32, (1, two, two), 1)
    col = lax.broadcasted_iota(jnp.int32, (1, two, two), 2)
    same = (row // CHUNK) == (col // CHUNK)
    fwd = z == 0
    ahead = (row % CHUNK) - (col % CHUNK)
    ahead = jnp.where(fwd, ahead, -ahead)
    strict = same & (ahead > 0)
    incl = same & (ahead >= 0)
    eye = (row == col).astype(F32)

    gram = _bmm_nt(jnp.concatenate([kq, rq], axis=1), jnp.concatenate([kd, bk], axis=1))
    a_kk = jnp.where(strict, gram[:, :two, :two], 0.0)
    a_kb = jnp.where(strict, gram[:, :two, two:], 0.0)
    a_rk = jnp.where(incl, gram[:, two:, :two], 0.0)
    a_rb = jnp.where(incl, gram[:, two:, two:], 0.0)

    pw = -a_kb
    tinv = eye + pw
    for _ in range(5):
        pw = _bmm(pw, pw)
        tinv = tinv + _bmm(pw, tinv)

    wu = _bmm(tinv, jnp.concatenate([kq, _bmm(a_kk, v)], axis=2))
    bkg_t = jnp.swapaxes(bk * egl, 1, 2)
    kdg_t = jnp.swapaxes(kd * egl, 1, 2)
    bwu = _bmm(bkg_t, wu)
    mc_s[...] = (eye * egl - bwu[:, :, :LANES]).astype(BF16)
    cc_s[...] = _bmm(kdg_t, v) - bwu[:, :, LANES:]
    rwu = _bmm(a_rb, wu)
    q = rq - rwu[:, :, :LANES]
    y0 = _bmm(a_rk, v) - rwu[:, :, LANES:]

    def body(ci, carry):
        cidx = jnp.where(fwd, ci, cg - 1 - ci)
        hb = h_s[...].astype(BF16)
        hs_s[cidx] = hb
        h_s[...] = _dot(mc_s[cidx], hb) + cc_s[cidx]
        return carry

    lax.fori_loop(0, cg, body, 0)
    y = _bmm(q, hs_s[...]) + y0
    y_ref[0, 0] = (y[:, :CHUNK, :] + y[:, CHUNK:, :]).reshape(cg * CHUNK, LANES).astype(y_ref.dtype)


def _rwkv_scan(kq, rq, kd, bk, v, egl, cg=8):
    _, B, T, c = kq.shape
    n_pairs = c // LANES
    tl = cg * CHUNK
    ng = T // tl
    assert T % tl == 0

    def gidx(z, g):
        return g + z * (ng - 1 - 2 * g)

    dir_in = pl.BlockSpec((1, 1, tl, LANES), lambda z, b, hp, g: (z, b, gidx(z, g), hp))
    kernel = functools.partial(_scan_kernel, cg=cg)
    return pl.pallas_call(
        kernel,
        grid=(2, B, n_pairs, ng),
        in_specs=[dir_in, dir_in, dir_in, dir_in,
                  pl.BlockSpec((1, tl, LANES), lambda z, b, hp, g: (b, gidx(z, g), hp)),
                  pl.BlockSpec((1, 1, cg, 1, LANES), lambda z, b, hp, g: (z, b, gidx(z, g), 0, hp))],
        out_specs=pl.BlockSpec((1, 1, tl, LANES), lambda z, b, hp, g: (z, b, gidx(z, g), hp)),
        out_shape=jax.ShapeDtypeStruct((2, B, T, c), BF16),
        scratch_shapes=[pltpu.VMEM((LANES, LANES), F32),
                        pltpu.VMEM((cg, LANES, LANES), BF16),
                        pltpu.VMEM((cg, LANES, LANES), F32),
                        pltpu.VMEM((cg, LANES, LANES), BF16)],
        compiler_params=_cparams(("parallel", "parallel", "parallel", "arbitrary")),
        name="rwkv_scan",
    )(kq, rq, kd, bk, v, egl)


def _mix_kernel(x_ref, att_ref, y_ref, bonus_ref, gate_ref, anw_ref, lnw_ref, lnb_ref, bd_ref, wo_ref,
                n2w_ref, rw_ref, rb_ref, h_ref, hn_ref, idx_ref, gates_ref, *, c):
    att = att_ref[...].astype(F32)
    ms = jnp.mean(att * att, axis=-1, keepdims=True)
    att_n = att * lax.rsqrt(ms + RMS_EPS) * anw_ref[...]

    y = y_ref[0].astype(F32) + y_ref[1].astype(F32)
    bd = bd_ref[...]
    inv_n = 1.0 / HEAD_DIM
    mean = _dot_exact_rhs(y, bd) * inv_n
    yc = y - mean
    var = _dot_exact_rhs(yc * yc, bd) * inv_n
    yn = yc * lax.rsqrt(var + GN_EPS) * lnw_ref[...] + lnb_ref[...]
    rk = (yn + bonus_ref[...].astype(F32)) * gate_ref[...].astype(F32)

    att_w = att.shape[-1]
    mix = _dot(att_n.astype(BF16), wo_ref[:att_w, :]) + _dot(rk.astype(BF16), wo_ref[att_w:, :])
    h = x_ref[...] + mix
    h_ref[...] = h
    ms2 = jnp.mean(h * h, axis=-1, keepdims=True)
    hn = h * lax.rsqrt(ms2 + RMS_EPS) * n2w_ref[...]
    hn_ref[...] = hn

    logits = jnp.dot(hn, rw_ref[...], preferred_element_type=F32, precision=lax.Precision.HIGHEST) + rb_ref[...]
    lane = lax.broadcasted_iota(jnp.int32, logits.shape, 1)
    cur = logits
    vals, idxs = [], []
    for _ in range(TOP_K):
        m = jnp.max(cur, axis=-1, keepdims=True)
        sel = jnp.min(jnp.where(cur == m, lane, LANES), axis=-1, keepdims=True)
        vals.append(m)
        idxs.append(sel)
        cur = jnp.where(lane == sel, -jnp.inf, cur)
    es = [jnp.exp(vk - vals[0]) for vk in vals]
    tot = es[0] + es[1] + es[2] + es[3]
    idx_out = jnp.zeros(logits.shape, jnp.int32)
    g_out = jnp.zeros(logits.shape, F32)
    for kk in range(TOP_K):
        idx_out = jnp.where(lane == kk, idxs[kk], idx_out)
        g_out = jnp.where(lane == kk, es[kk] / tot, g_out)
    idx_ref[...] = idx_out
    gates_ref[...] = g_out


def _mix(x2d, att, y, bonus, gate, attn_norm_w, ln_w, ln_b, w_o, norm2_w, router_w, router_b, tm=256):
    M, D = x2d.shape
    c = y.shape[-1]
    att_w = att.shape[-1]
    ch = np.arange(c) // HEAD_DIM
    bd = jnp.asarray((ch[:, None] == ch[None, :]).astype(np.float32), BF16)
    rw = jnp.zeros((D, LANES), F32).at[:, :N_EXPERTS].set(router_w)
    rb = jnp.full((1, LANES), -jnp.inf, F32).at[0, :N_EXPERTS].set(router_b)
    row = lambda u: u.reshape(1, -1).astype(F32)
    const = lambda shape: pl.BlockSpec(shape, lambda i: (0,) * len(shape))
    kernel = functools.partial(_mix_kernel, c=c)
    return pl.pallas_call(
        kernel,
        grid=(M // tm,),
        in_specs=[
            pl.BlockSpec((tm, D), lambda i: (i, 0)),
            pl.BlockSpec((tm, att_w), lambda i: (i, 0)),
            pl.BlockSpec((2, tm, c), lambda i: (0, i, 0)),
            pl.BlockSpec((tm, c), lambda i: (i, 0)),
            pl.BlockSpec((tm, c), lambda i: (i, 0)),
            const((1, att_w)), const((1, c)), const((1, c)), const((c, c)), const((att_w + c, D)),
            const((1, D)), const((D, LANES)), const((1, LANES)),
        ],
        out_specs=[
            pl.BlockSpec((tm, D), lambda i: (i, 0)),
            pl.BlockSpec((tm, D), lambda i: (i, 0)),
            pl.BlockSpec((tm, LANES), lambda i: (i, 0)),
            pl.BlockSpec((tm, LANES), lambda i: (i, 0)),
        ],
        out_shape=[jax.ShapeDtypeStruct((M, D), F32), jax.ShapeDtypeStruct((M, D), F32),
                   jax.ShapeDtypeStruct((M, LANES), jnp.int32), jax.ShapeDtypeStruct((M, LANES), F32)],
        compiler_params=_cparams(("parallel",)),
        name="mix_router",
    )(x2d, att, y, bonus, gate, row(attn_norm_w), row(ln_w), row(ln_b), bd, w_o.astype(BF16),
      row(norm2_w), rw, rb)


def _gather_kernel(idx_ref, src_ref, dst_ref, idx_smem, sem_idx, sem, *, gb):
    i = pl.program_id(0)
    cp = pltpu.make_async_copy(idx_ref.at[0], idx_smem, sem_idx)
    cp.start()
    cp.wait()
    base = i * gb

    def issue(j, carry):
        pltpu.make_async_copy(src_ref.at[pl.ds(idx_smem[0, j], 1)], dst_ref.at[pl.ds(base + j, 1)], sem).start()
        return carry

    lax.fori_loop(0, gb, issue, 0)
    pltpu.make_async_copy(src_ref.at[pl.ds(0, gb)], dst_ref.at[pl.ds(base, gb)], sem).wait()


def _gather_rows(src, idx, gb=512):
    n = idx.shape[0]
    assert n % gb == 0
    d = src.shape[1]
    kernel = functools.partial(_gather_kernel, gb=gb)
    return pl.pallas_call(
        kernel,
        grid=(n // gb,),
        in_specs=[pl.BlockSpec((1, 1, gb), lambda i: (i, 0, 0)),
                  pl.BlockSpec(memory_space=pl.ANY)],
        out_specs=pl.BlockSpec(memory_space=pl.ANY),
        out_shape=jax.ShapeDtypeStruct((n, d), src.dtype),
        scratch_shapes=[pltpu.SMEM((1, gb), jnp.int32), pltpu.SemaphoreType.DMA(()), pltpu.SemaphoreType.DMA(())],
        compiler_params=_cparams(("arbitrary",)),
        name="gather_rows",
    )(idx.reshape(n // gb, 1, gb).astype(jnp.int32), src)


def _expert_kernel(be_ref, x_ref, wu_ref, bu_ref, wd_ref, bdn_ref, y_ref, wu_s, wd_s, *, f):
    i = pl.program_id(0)
    e = be_ref[i]
    changed = jnp.logical_or(i == 0, e != be_ref[jnp.maximum(i - 1, 0)])

    @pl.when(changed)
    def _():
        wu_s[...] = wu_ref[0].astype(BF16)
        wd_s[...] = wd_ref[0].astype(BF16)

    x = x_ref[...].astype(BF16)
    hcat = _dot(x, wu_s[...]) + bu_ref[0]
    gate = jnp.minimum(hcat[:, :f], SWIGLU_LIMIT)
    lin = jnp.clip(hcat[:, f:], -SWIGLU_LIMIT, SWIGLU_LIMIT)
    act = (lin + 1.0) * (gate * jax.nn.sigmoid(SWIGLU_ALPHA * gate))
    y_ref[...] = _dot(act.astype(BF16), wd_s[...]) + bdn_ref[0]


def _experts(x_buf, blk_expert, w_up, b_up, w_down, b_down, blk):
    P, D = x_buf.shape
    E, _, f2 = w_up.shape
    f = f2 // 2
    kernel = functools.partial(_expert_kernel, f=f)
    grid_spec = pltpu.PrefetchScalarGridSpec(
        num_scalar_prefetch=1,
        grid=(P // blk,),
        in_specs=[
            pl.BlockSpec((blk, D), lambda i, be: (i, 0)),
            pl.BlockSpec((1, D, f2), lambda i, be: (be[i], 0, 0)),
            pl.BlockSpec((1, 1, f2), lambda i, be: (be[i], 0, 0)),
            pl.BlockSpec((1, f, D), lambda i, be: (be[i], 0, 0)),
            pl.BlockSpec((1, 1, D), lambda i, be: (be[i], 0, 0)),
        ],
        out_specs=pl.BlockSpec((blk, D), lambda i, be: (i, 0)),
        scratch_shapes=[pltpu.VMEM((D, f2), BF16), pltpu.VMEM((f, D), BF16)],
    )
    return pl.pallas_call(
        kernel,
        grid_spec=grid_spec,
        out_shape=jax.ShapeDtypeStruct((P, D), F32),
        compiler_params=_cparams(("arbitrary",)),
        name="experts",
    )(blk_expert, x_buf, w_up, b_up.reshape(E, 1, f2), w_down, b_down.reshape(E, 1, D))


def _combine_kernel(h_ref, yg_ref, g_ref, fw_ref, o_ref):
    g = g_ref[...]
    acc = h_ref[...]
    for kk in range(TOP_K):
        acc = acc + g[:, kk:kk + 1] * yg_ref[kk]
    ms = jnp.mean(acc * acc, axis=-1, keepdims=True)
    o_ref[...] = acc * lax.rsqrt(ms + RMS_EPS) * fw_ref[...]


def _combine(h, y_gath, gates, final_w, tm=256):
    M, D = h.shape
    return pl.pallas_call(
        _combine_kernel,
        grid=(M // tm,),
        in_specs=[
            pl.BlockSpec((tm, D), lambda i: (i, 0)),
            pl.BlockSpec((TOP_K, tm, D), lambda i: (0, i, 0)),
            pl.BlockSpec((tm, LANES), lambda i: (i, 0)),
            pl.BlockSpec((1, D), lambda i: (0, 0)),
        ],
        out_specs=pl.BlockSpec((tm, D), lambda i: (i, 0)),
        out_shape=jax.ShapeDtypeStruct((M, D), F32),
        compiler_params=_cparams(("parallel",)),
        name="combine",
    )(h, y_gath, gates, final_w.reshape(1, D).astype(F32))


def _route(top_idx, blk):
    M = top_idx.shape[0]
    A = M * TOP_K
    e_flat = top_idx.reshape(A)
    onehot = (e_flat[:, None] == jnp.arange(N_EXPERTS, dtype=jnp.int32)[None, :]).astype(jnp.int32)
    csum = jnp.cumsum(onehot, axis=0)
    counts = csum[-1]
    rank = jnp.take_along_axis(csum, e_flat[:, None], axis=1)[:, 0] - 1
    padded = (counts + blk - 1) // blk * blk
    pend = jnp.cumsum(padded)
    pstart = pend - padded
    dest = pstart[e_flat] + rank
    P = (A + N_EXPERTS * blk) // blk * blk
    n_blk = P // blk
    tok = jnp.arange(A, dtype=jnp.int32) // TOP_K
    row_tok = jnp.zeros((P,), jnp.int32).at[dest].set(tok)
    blk_expert = jnp.searchsorted(pend, jnp.arange(n_blk, dtype=jnp.int32) * blk, side="right")
    blk_expert = jnp.minimum(blk_expert, N_EXPERTS - 1).astype(jnp.int32)
    return row_tok, dest.astype(jnp.int32), blk_expert


def kernel(x, norm1_w, w_in, attn_rpb, attn_norm_w, rwkv_mu_prev, rwkv_mu_next, rwkv_w0, rwkv_w2, rwkv_a0, rwkv_a2, rwkv_g2, rwkv_k_k, rwkv_k_a, rwkv_r_k, rwkv_ln_w, rwkv_ln_b, w_o, norm2_w, router_w, router_b, expert_w_up, expert_b_up, expert_w_down, expert_b_down, final_norm_w):
    B, T, D = x.shape
    M = B * T
    depth = norm1_w.shape[0]
    assert depth == 1, "the combine stage folds in the final norm, so it must follow the only layer"
    c = rwkv_k_k.shape[-1]
    att_w = attn_norm_w.shape[-1]
    blk = 256
    h = x.reshape(M, D)
    for l in range(depth):
        qkv, p = _in_proj(h, norm1_w[l], w_in[l].astype(BF16), 3 * att_w)
        att = _na2d(qkv.reshape(B, T, 3 * att_w), _na_bias_table(attn_rpb[l]), att_w)
        kq, rq, kd, bk, egl, v, bonus, gate = _rwkv_prep(
            p.reshape(B, T, -1), rwkv_mu_prev[l], rwkv_mu_next[l], rwkv_w0[l].reshape(-1), rwkv_w2[l],
            rwkv_a0[l].reshape(-1), rwkv_a2[l], rwkv_g2[l], rwkv_k_k[l], rwkv_k_a[l], rwkv_r_k[l].reshape(-1))
        y = _rwkv_scan(kq, rq, kd, bk, v, egl)
        h, hn, idx_pad, gates_pad = _mix(
            h, att.reshape(M, att_w), y.reshape(2, M, c), bonus.reshape(M, c), gate.reshape(M, c),
            attn_norm_w[l], rwkv_ln_w[l], rwkv_ln_b[l], w_o[l], norm2_w[l], router_w[l], router_b[l])
        row_tok, dest, blk_expert = _route(idx_pad[:, :TOP_K], blk)
        x_buf = _gather_rows(hn, row_tok)
        y_buf = _experts(x_buf, blk_expert, expert_w_up[l], expert_b_up[l], expert_w_down[l], expert_b_down[l], blk)
        y_gath = _gather_rows(y_buf, dest.reshape(M, TOP_K).T.reshape(-1)).reshape(TOP_K, M, D)
        h = _combine(h, y_gath, gates_pad, final_norm_w)
    return h.reshape(B, T, D)
```

```python
import functools

import numpy as np
import jax
import jax.numpy as jnp
from jax import lax
from jax.experimental import pallas as pl
from jax.experimental.pallas import tpu as pltpu

F32 = jnp.float32
BF16 = jnp.bfloat16

HEAD_DIM = 64
GRID_W = 64
WIN_H = 8
WIN_W = 16
N_EXPERTS = 32
TOP_K = 4
SWIGLU_ALPHA = 1.702
SWIGLU_LIMIT = 7.0
RMS_EPS = 1e-5
GN_EPS = 64e-5
D_LORA = 64
D_GATE_LORA = 128
CHUNK = 64
LANES = 128
MASK_NEG = -1e30
VMEM_LIMIT = 56 * 1024 * 1024


def _cparams(sem):
    return pltpu.CompilerParams(dimension_semantics=sem, vmem_limit_bytes=VMEM_LIMIT)


def _dot(a, b):
    return jnp.dot(a, b, preferred_element_type=F32)


def _split3(x):
    hi = x.astype(BF16)
    r1 = x - hi.astype(F32)
    mid = r1.astype(BF16)
    lo = (r1 - mid.astype(F32)).astype(BF16)
    return hi, mid, lo


def _dot_exact_lhs(a01, x):
    hi, mid, lo = _split3(x)
    return _dot(a01, hi) + _dot(a01, mid) + _dot(a01, lo)


def _dot_exact_rhs(x, b01):
    hi, mid, lo = _split3(x)
    return _dot(hi, b01) + _dot(mid, b01) + _dot(lo, b01)


def _inproj_kernel(x_ref, nw_ref, w_ref, qkv_ref, p_ref):
    x = x_ref[...]
    ms = jnp.mean(x * x, axis=-1, keepdims=True)
    hn = (x * lax.rsqrt(ms + RMS_EPS) * nw_ref[...]).astype(BF16)
    n_qkv = qkv_ref.shape[-1]
    qkv_ref[...] = _dot(hn, w_ref[:, :n_qkv]).astype(BF16)
    p_ref[...] = _dot(hn, w_ref[:, n_qkv:]).astype(BF16)


def _in_proj(x2d, norm_w, w_in_bf16, n_qkv, tm=512):
    M, D = x2d.shape
    n_all = w_in_bf16.shape[1]
    n_p = n_all - n_qkv
    return pl.pallas_call(
        _inproj_kernel,
        grid=(M // tm,),
        in_specs=[
            pl.BlockSpec((tm, D), lambda i: (i, 0)),
            pl.BlockSpec((1, D), lambda i: (0, 0)),
            pl.BlockSpec((D, n_all), lambda i: (0, 0)),
        ],
        out_specs=[
            pl.BlockSpec((tm, n_qkv), lambda i: (i, 0)),
            pl.BlockSpec((tm, n_p), lambda i: (i, 0)),
        ],
        out_shape=[jax.ShapeDtypeStruct((M, n_qkv), BF16), jax.ShapeDtypeStruct((M, n_p), BF16)],
        compiler_params=_cparams(("parallel",)),
        name="in_proj",
    )(x2d, norm_w.reshape(1, D), w_in_bf16)


def _na_bias_table(rpb):
    H = rpb.shape[0]
    cols = np.arange(GRID_W)
    col_start = np.clip(cols - WIN_W // 2, 0, GRID_W - WIN_W)
    kc = np.arange(GRID_W)
    valid = (kc[None, :] >= col_start[:, None]) & (kc[None, :] < col_start[:, None] + WIN_W)
    dc = kc[None, :] - cols[:, None] + (WIN_W - 1)
    var = np.arange(WIN_H)
    wi = np.arange(WIN_H)
    dr = wi[None, :] - var[:, None] + (WIN_H - 1)
    row_sel = (dr[:, :, None] == np.arange(2 * WIN_H - 1)).astype(np.float32)
    col_sel = ((dc[:, :, None] == np.arange(2 * WIN_W - 1)) & valid[:, :, None]).astype(np.float32)
    tab = jnp.einsum("vir,hrd,xyd->vhxiy", row_sel, rpb.astype(F32), col_sel, precision=lax.Precision.HIGHEST)
    tab = jnp.where(valid[None, None, :, None, :], tab, MASK_NEG)
    return tab.reshape(WIN_H, H, GRID_W, WIN_H * GRID_W)


def _na_kernel(q_ref, k_ref, v_ref, bias_ref, o_ref, *, rows_per_step, n_rows):
    i = pl.program_id(2)
    lane = lax.broadcasted_iota(jnp.int32, (1, LANES), 1)
    head0 = lane < HEAD_DIM
    scale = HEAD_DIM ** -0.5
    nkeys = WIN_H * GRID_W
    for rr in range(rows_per_step):
        r = i * rows_per_step + rr
        r_start = jnp.clip(r - WIN_H // 2, 0, n_rows - WIN_H)
        var = r - r_start
        q = q_ref[0, rr * GRID_W:(rr + 1) * GRID_W, :]
        koff = pl.multiple_of(r_start * GRID_W, GRID_W)
        kwin = k_ref[0, pl.ds(koff, nkeys), :]
        vwin = v_ref[0, pl.ds(koff, nkeys), :]
        outs = []
        for h in range(2):
            qh = jnp.where(head0 if h == 0 else jnp.logical_not(head0), q, jnp.zeros_like(q))
            s = lax.dot_general(qh, kwin, (((1,), (1,)), ((), ())), preferred_element_type=F32)
            s = s * scale + bias_ref[var, h]
            m = jnp.max(s, axis=-1, keepdims=True)
            e = jnp.exp(s - m)
            denom = jnp.sum(e, axis=-1, keepdims=True)
            o = _dot(e.astype(BF16), vwin) / denom
            outs.append(o)
        o_ref[0, rr * GRID_W:(rr + 1) * GRID_W, :] = jnp.where(head0, outs[0], outs[1]).astype(o_ref.dtype)


def _na2d(qkv, bias_tab, att_w, rows_per_step=8):
    B, T, _ = qkv.shape
    n_rows = T // GRID_W
    assert n_rows >= WIN_H and n_rows % rows_per_step == 0
    n_pairs = att_w // LANES
    tq = rows_per_step * GRID_W
    kernel = functools.partial(_na_kernel, rows_per_step=rows_per_step, n_rows=n_rows)
    return pl.pallas_call(
        kernel,
        grid=(B, n_pairs, n_rows // rows_per_step),
        in_specs=[
            pl.BlockSpec((1, tq, LANES), lambda b, hp, i: (b, i, hp)),
            pl.BlockSpec((1, T, LANES), lambda b, hp, i: (b, 0, n_pairs + hp)),
            pl.BlockSpec((1, T, LANES), lambda b, hp, i: (b, 0, 2 * n_pairs + hp)),
            pl.BlockSpec((WIN_H, 2, GRID_W, WIN_H * GRID_W), lambda b, hp, i: (0, hp, 0, 0)),
        ],
        out_specs=pl.BlockSpec((1, tq, LANES), lambda b, hp, i: (b, i, hp)),
        out_shape=jax.ShapeDtypeStruct((B, T, att_w), BF16),
        compiler_params=_cparams(("parallel", "parallel", "arbitrary")),
        name="na2d",
    )(qkv, qkv, qkv, bias_tab)


def _prep_kernel(p_ref, pprev_ref, pnext_ref, mup_ref, mun_ref, w0_ref, w2_ref, a0_ref, a2_ref, g2_ref,
                 kk_ref, ka_ref, rk_ref, bd_ref, tri_ref, ones_ref,
                 kq_ref, rq_ref, kd_ref, bk_ref, egl_ref, v_ref, bonus_ref, gate_ref, *, tb, c):
    i = pl.program_id(1)
    nb = pl.num_programs(1)
    p = p_ref[0].astype(F32)
    halo = pprev_ref.shape[1]
    prow = jnp.where(i > 0, pprev_ref[0, halo - 1:halo, :].astype(F32), 0.0)
    nrow = jnp.where(i < nb - 1, pnext_ref[0, 0:1, :].astype(F32), 0.0)
    ridx = lax.broadcasted_iota(jnp.int32, (tb, 1), 0)
    p_prev = jnp.where(ridx == 0, prow, pltpu.roll(p, 1, 0))
    p_next = jnp.where(ridx == tb - 1, nrow, pltpu.roll(p, tb - 1, 0))
    pm = p + mup_ref[...] * (p_prev - p) + mun_ref[...] * (p_next - p)
    r = pm[:, :c]
    k = pm[:, c:2 * c]
    v = pm[:, 2 * c:3 * c]
    o = 3 * c
    dw = pm[:, o:o + 2 * D_LORA]
    da = pm[:, o + 2 * D_LORA:o + 4 * D_LORA]
    dg = pm[:, o + 4 * D_LORA:o + 4 * D_LORA + D_GATE_LORA]

    wl = w0_ref[...] + _dot(jnp.tanh(dw).astype(BF16), w2_ref[...])
    neg = -wl
    softplus = jnp.maximum(neg, 0.0) + jnp.log(1.0 + jnp.exp(-jnp.abs(neg)))
    logw = -jnp.exp(-softplus - 0.5)
    a = jax.nn.sigmoid(a0_ref[...] + _dot(da.astype(BF16), a2_ref[...]))
    gate = _dot(jax.nn.sigmoid(dg).astype(BF16), g2_ref[...])

    bd = bd_ref[...]
    kkr = k * kk_ref[...]
    ssq = _dot_exact_rhs(kkr * kkr, bd)
    kk = kkr / jnp.maximum(jnp.sqrt(ssq), 1e-12)

    ksum = jnp.zeros_like(k)
    nchunk = tb // CHUNK
    for z in range(2):
        a_z = a[:, z * c:(z + 1) * c]
        lw = logw[:, z * c:(z + 1) * c]
        kdir = k * (1.0 + (a_z - 1.0) * ka_ref[...])
        ksum = ksum + kdir
        b_z = a_z * kk
        g_incl = _dot_exact_lhs(tri_ref[z], lw)
        g_all = _dot_exact_lhs(ones_ref[...], lw)
        e_neg = jnp.exp(-g_incl)
        kq_ref[z, 0] = (kk * jnp.exp(g_incl - lw)).astype(BF16)
        rq_ref[z, 0] = (r * jnp.exp(g_incl)).astype(BF16)
        kd_ref[z, 0] = (kdir * e_neg).astype(BF16)
        bk_ref[z, 0] = (b_z * e_neg).astype(BF16)
        egl_ref[z, 0] = jnp.exp(g_all).reshape(nchunk, CHUNK, c)[:, 0:1, :]
    coef = _dot_exact_rhs(r * ksum * rk_ref[...], bd)
    v_ref[0] = v.astype(BF16)
    bonus_ref[0] = (coef * v).astype(BF16)
    gate_ref[0] = gate.astype(BF16)


def _rwkv_prep(p, mu_prev, mu_next, w0, w2, a0, a2, g2, k_k, k_a, r_k, tb=256):
    B, T, pw = p.shape
    c = k_k.shape[0]
    assert T % tb == 0 and tb % CHUNK == 0
    halo = 16
    nhb = T // halo
    nchunk = tb // CHUNK
    w2bd = jnp.zeros((2 * D_LORA, 2 * c), F32)
    a2bd = jnp.zeros((2 * D_LORA, 2 * c), F32)
    for z in range(2):
        w2bd = w2bd.at[z * D_LORA:(z + 1) * D_LORA, z * c:(z + 1) * c].set(w2[z])
        a2bd = a2bd.at[z * D_LORA:(z + 1) * D_LORA, z * c:(z + 1) * c].set(a2[z])
    ch = np.arange(c) // HEAD_DIM
    bd = jnp.asarray((ch[:, None] == ch[None, :]).astype(np.float32), BF16)
    t = np.arange(tb)
    same = (t[:, None] // CHUNK) == (t[None, :] // CHUNK)
    tri = np.stack([same & (t[None, :] <= t[:, None]), same & (t[None, :] >= t[:, None])]).astype(np.float32)
    tri = jnp.asarray(tri, BF16)
    ones = jnp.asarray(same.astype(np.float32), BF16)
    row = lambda u: u.reshape(1, -1).astype(F32)
    const = lambda shape: pl.BlockSpec(shape, lambda b, i: (0,) * len(shape))
    dir_out = pl.BlockSpec((2, 1, tb, c), lambda b, i: (0, b, i, 0))
    tok_out = pl.BlockSpec((1, tb, c), lambda b, i: (b, i, 0))
    kernel = functools.partial(_prep_kernel, tb=tb, c=c)
    return pl.pallas_call(
        kernel,
        grid=(B, T // tb),
        in_specs=[
            pl.BlockSpec((1, tb, pw), lambda b, i: (b, i, 0)),
            pl.BlockSpec((1, halo, pw), lambda b, i: (b, jnp.maximum(i * (tb // halo) - 1, 0), 0)),
            pl.BlockSpec((1, halo, pw), lambda b, i: (b, jnp.minimum((i + 1) * (tb // halo), nhb - 1), 0)),
            const((1, pw)), const((1, pw)),
            const((1, 2 * c)), const((2 * D_LORA, 2 * c)),
            const((1, 2 * c)), const((2 * D_LORA, 2 * c)),
            const((D_GATE_LORA, c)),
            const((1, c)), const((1, c)), const((1, c)),
            const((c, c)), const((2, tb, tb)), const((tb, tb)),
        ],
        out_specs=[dir_out, dir_out, dir_out, dir_out,
                   pl.BlockSpec((2, 1, nchunk, 1, c), lambda b, i: (0, b, i, 0, 0)),
                   tok_out, tok_out, tok_out],
        out_shape=[jax.ShapeDtypeStruct((2, B, T, c), BF16)] * 4
        + [jax.ShapeDtypeStruct((2, B, T // CHUNK, 1, c), F32)]
        + [jax.ShapeDtypeStruct((B, T, c), BF16)] * 3,
        compiler_params=_cparams(("parallel", "parallel")),
        name="rwkv_prep",
    )(p, p, p, row(mu_prev), row(mu_next), row(w0), w2bd.astype(BF16), row(a0), a2bd.astype(BF16),
      g2.astype(BF16), row(k_k), row(k_a), row(r_k), bd, tri, ones)


def _bmm(a, b):
    return jnp.einsum("cij,cjk->cik", a.astype(BF16), b.astype(BF16), preferred_element_type=F32)


def _bmm_nt(a, b):
    return jnp.einsum("cik,cjk->cij", a.astype(BF16), b.astype(BF16), preferred_element_type=F32)


def _scan_kernel(kq_ref, rq_ref, kd_ref, bk_ref, v_ref, egl_ref, y_ref, h_s, mc_s, cc_s, hs_s, *, cg):
    z = pl.program_id(0)
    gi = pl.program_id(3)

    @pl.when(gi == 0)
    def _():
        h_s[...] = jnp.zeros_like(h_s)

    two = 2 * CHUNK
    lane = lax.broadcasted_iota(jnp.int32, (1, 1, LANES), 2)
    head0 = lane < HEAD_DIM

    def stacked(x):
        x = x.astype(F32).reshape(cg, CHUNK, LANES)
        return jnp.concatenate([jnp.where(head0, x, 0.0), jnp.where(head0, 0.0, x)], axis=1)

    kq = stacked(kq_ref[0, 0])
    rq = stacked(rq_ref[0, 0])
    kd = stacked(kd_ref[0, 0])
    bk = stacked(bk_ref[0, 0])
    v = stacked(v_ref[0])
    egl = egl_ref[0, 0]

    row = lax.broadcasted_iota(jnp.int32, (1, two, two), 1)
    col = lax.broadcasted_iota(jnp.int32, (1, two, two), 2)
    same = (row // CHUNK) == (col // CHUNK)
    fwd = z == 0
    ahead = (row % CHUNK) - (col % CHUNK)
    ahead = jnp.where(fwd, ahead, -ahead)
    strict = same & (ahead > 0)
    incl = same & (ahead >= 0)
    eye = (row == col).astype(F32)

    gram = _bmm_nt(jnp.concatenate([kq, rq], axis=1), jnp.concatenate([kd, bk], axis=1))
    a_kk = jnp.where(strict, gram[:, :two, :two], 0.0)
    a_kb = jnp.where(strict, gram[:, :two, two:], 0.0)
    a_rk = jnp.where(incl, gram[:, two:, :two], 0.0)
    a_rb = jnp.where(incl, gram[:, two:, two:], 0.0)

    pw = -a_kb
    tinv = eye + pw
    for _ in range(5):
        pw = _bmm(pw, pw)
        tinv = tinv + _bmm(pw, tinv)

    wu = _bmm(tinv, jnp.concatenate([kq, _bmm(a_kk, v)], axis=2))
    bkg_t = jnp.swapaxes(bk * egl, 1, 2)
    kdg_t = jnp.swapaxes(kd * egl, 1, 2)
    bwu = _bmm(bkg_t, wu)
    mc_s[...] = (eye * egl - bwu[:, :, :LANES]).astype(BF16)
    cc_s[...] = _bmm(kdg_t, v) - bwu[:, :, LANES:]
    rwu = _bmm(a_rb, wu)
    q = rq - rwu[:, :, :LANES]
    y0 = _bmm(a_rk, v) - rwu[:, :, LANES:]

    def body(ci, carry):
        cidx = jnp.where(fwd, ci, cg - 1 - ci)
        hb = h_s[...].astype(BF16)
        hs_s[cidx] = hb
        h_s[...] = _dot(mc_s[cidx], hb) + cc_s[cidx]
        return carry

    lax.fori_loop(0, cg, body, 0)
    y = _bmm(q, hs_s[...]) + y0
    y_ref[0, 0] = (y[:, :CHUNK, :] + y[:, CHUNK:, :]).reshape(cg * CHUNK, LANES).astype(y_ref.dtype)


def _rwkv_scan(kq, rq, kd, bk, v, egl, cg=8):
    _, B, T, c = kq.shape
    n_pairs = c // LANES
    tl = cg * CHUNK
    ng = T // tl
    assert T % tl == 0

    def gidx(z, g):
        return g + z * (ng - 1 - 2 * g)

    dir_in = pl.BlockSpec((1, 1, tl, LANES), lambda z, b, hp, g: (z, b, gidx(z, g), hp))
    kernel = functools.partial(_scan_kernel, cg=cg)
    return pl.pallas_call(
        kernel,
        grid=(2, B, n_pairs, ng),
        in_specs=[dir_in, dir_in, dir_in, dir_in,
                  pl.BlockSpec((1, tl, LANES), lambda z, b, hp, g: (b, gidx(z, g), hp)),
                  pl.BlockSpec((1, 1, cg, 1, LANES), lambda z, b, hp, g: (z, b, gidx(z, g), 0, hp))],
        out_specs=pl.BlockSpec((1, 1, tl, LANES), lambda z, b, hp, g: (z, b, gidx(z, g), hp)),
        out_shape=jax.ShapeDtypeStruct((2, B, T, c), BF16),
        scratch_shapes=[pltpu.VMEM((LANES, LANES), F32),
                        pltpu.VMEM((cg, LANES, LANES), BF16),
                        pltpu.VMEM((cg, LANES, LANES), F32),
                        pltpu.VMEM((cg, LANES, LANES), BF16)],
        compiler_params=_cparams(("parallel", "parallel", "parallel", "arbitrary")),
        name="rwkv_scan",
    )(kq, rq, kd, bk, v, egl)


def _mix_kernel(x_ref, att_ref, y_ref, bonus_ref, gate_ref, anw_ref, lnw_ref, lnb_ref, bd_ref, wo_ref,
                n2w_ref, rw_ref, rb_ref, tril_ref, h_ref, hn_ref, idx_ref, gates_ref, rank_ref, cnt_ref, cnt_s):
    @pl.when(pl.program_id(0) == 0)
    def _():
        cnt_s[...] = jnp.zeros_like(cnt_s)

    att = att_ref[...].astype(F32)
    ms = jnp.mean(att * att, axis=-1, keepdims=True)
    att_n = att * lax.rsqrt(ms + RMS_EPS) * anw_ref[...]

    y = y_ref[0].astype(F32) + y_ref[1].astype(F32)
    bd = bd_ref[...]
    inv_n = 1.0 / HEAD_DIM
    mean = _dot_exact_rhs(y, bd) * inv_n
    yc = y - mean
    var = _dot_exact_rhs(yc * yc, bd) * inv_n
    yn = yc * lax.rsqrt(var + GN_EPS) * lnw_ref[...] + lnb_ref[...]
    rk = (yn + bonus_ref[...].astype(F32)) * gate_ref[...].astype(F32)

    att_w = att.shape[-1]
    mix = _dot(att_n.astype(BF16), wo_ref[:att_w, :]) + _dot(rk.astype(BF16), wo_ref[att_w:, :])
    h = x_ref[...] + mix
    h_ref[...] = h
    ms2 = jnp.mean(h * h, axis=-1, keepdims=True)
    hn = h * lax.rsqrt(ms2 + RMS_EPS) * n2w_ref[...]
    for s in range(hn_ref.shape[1]):
        hn_ref[:, s, :] = hn[:, s * LANES:(s + 1) * LANES]

    logits = jnp.dot(hn, rw_ref[...], preferred_element_type=F32, precision=lax.Precision.HIGHEST) + rb_ref[...]
    lane = lax.broadcasted_iota(jnp.int32, logits.shape, 1)
    cur = logits
    vals, idxs = [], []
    for _ in range(TOP_K):
        m = jnp.max(cur, axis=-1, keepdims=True)
        sel = jnp.min(jnp.where(cur == m, lane, LANES), axis=-1, keepdims=True)
        vals.append(m)
        idxs.append(sel)
        cur = jnp.where(lane == sel, -jnp.inf, cur)
    es = [jnp.exp(vk - vals[0]) for vk in vals]
    tot = es[0] + es[1] + es[2] + es[3]
    onehot = jnp.zeros(logits.shape, F32)
    for kk in range(TOP_K):
        onehot = onehot + (lane == idxs[kk]).astype(F32)
    before = cnt_s[...] + _dot(tril_ref[...], onehot.astype(BF16))
    cnt_s[...] = cnt_s[...] + jnp.sum(onehot, axis=0, keepdims=True)
    cnt_ref[...] = cnt_s[...]

    idx_out = jnp.zeros(logits.shape, jnp.int32)
    g_out = jnp.zeros(logits.shape, F32)
    rank_out = jnp.zeros(logits.shape, F32)
    for kk in range(TOP_K):
        idx_out = jnp.where(lane == kk, idxs[kk], idx_out)
        g_out = jnp.where(lane == kk, es[kk] / tot, g_out)
        rank_k = jnp.sum(jnp.where(lane == idxs[kk], before, 0.0), axis=-1, keepdims=True)
        rank_out = jnp.where(lane == kk, rank_k, rank_out)
    idx_ref[...] = idx_out
    gates_ref[...] = g_out
    rank_ref[...] = rank_out.astype(jnp.int32)


def _mix(x2d, att, y, bonus, gate, attn_norm_w, ln_w, ln_b, w_o, norm2_w, router_w, router_b, tm=256):
    M, D = x2d.shape
    c = y.shape[-1]
    att_w = att.shape[-1]
    ch = np.arange(c) // HEAD_DIM
    bd = jnp.asarray((ch[:, None] == ch[None, :]).astype(np.float32), BF16)
    t = np.arange(tm)
    tril = jnp.asarray((t[None, :] < t[:, None]).astype(np.float32), BF16)
    rw = jnp.zeros((D, LANES), F32).at[:, :N_EXPERTS].set(router_w)
    rb = jnp.full((1, LANES), -jnp.inf, F32).at[0, :N_EXPERTS].set(router_b)
    row = lambda u: u.reshape(1, -1).astype(F32)
    const = lambda shape: pl.BlockSpec(shape, lambda i: (0,) * len(shape))
    tok = lambda w: pl.BlockSpec((tm, w), lambda i: (i, 0))
    return pl.pallas_call(
        _mix_kernel,
        grid=(M // tm,),
        in_specs=[
            tok(D), tok(att_w),
            pl.BlockSpec((2, tm, c), lambda i: (0, i, 0)),
            tok(c), tok(c),
            const((1, att_w)), const((1, c)), const((1, c)), const((c, c)), const((att_w + c, D)),
            const((1, D)), const((D, LANES)), const((1, LANES)), const((tm, tm)),
        ],
        out_specs=[
            tok(D),
            pl.BlockSpec((tm, D // LANES, LANES), lambda i: (i, 0, 0)),
            tok(LANES), tok(LANES), tok(LANES),
            const((1, LANES)),
        ],
        out_shape=[jax.ShapeDtypeStruct((M, D), F32), jax.ShapeDtypeStruct((M, D // LANES, LANES), F32),
                   jax.ShapeDtypeStruct((M, LANES), jnp.int32), jax.ShapeDtypeStruct((M, LANES), F32),
                   jax.ShapeDtypeStruct((M, LANES), jnp.int32), jax.ShapeDtypeStruct((1, LANES), F32)],
        scratch_shapes=[pltpu.VMEM((1, LANES), F32)],
        compiler_params=_cparams(("arbitrary",)),
        name="mix_router",
    )(x2d, att, y, bonus, gate, row(attn_norm_w), row(ln_w), row(ln_b), bd, w_o.astype(BF16),
      row(norm2_w), rw, rb, tril)


def _route(top_idx, rank, counts, blk):
    M = top_idx.shape[0]
    counts = counts.astype(jnp.int32)
    padded = (counts + blk - 1) // blk * blk
    pend = jnp.cumsum(padded)
    pstart = pend - padded
    experts = jnp.arange(N_EXPERTS, dtype=jnp.int32)
    first = jnp.sum(jnp.where(top_idx[:, :, None] == experts, pstart, 0), axis=-1)
    dest = (first + rank).astype(jnp.int32)
    n_blk = (M * TOP_K + N_EXPERTS * blk) // blk
    starts = jnp.arange(n_blk, dtype=jnp.int32) * blk
    blk_expert = jnp.sum((pend[None, :] <= starts[:, None]).astype(jnp.int32), axis=1)
    blk_expert = jnp.minimum(blk_expert, N_EXPERTS - 1).astype(jnp.int32)
    n_used = (pend[-1] // blk).reshape(1).astype(jnp.int32)
    return dest, counts, padded.astype(jnp.int32), pstart.astype(jnp.int32), blk_expert, n_used


def _dispatch_kernel(cnt_ref, pad_ref, first_ref, dest_ref, hn_ref, xb_ref, dest_smem, zero_s, sem_idx, sem, sem_pad,
                     *, tm):
    i = pl.program_id(0)
    cp = pltpu.make_async_copy(dest_ref.at[0], dest_smem, sem_idx)
    cp.start()
    cp.wait()

    def issue(t, carry):
        for kk in range(TOP_K):
            pltpu.make_async_copy(hn_ref.at[t], xb_ref.at[dest_smem[0, t * TOP_K + kk]], sem).start()
        return carry

    lax.fori_loop(0, tm, issue, 0)

    @pl.when(i == 0)
    def _():
        zero_s[...] = jnp.zeros_like(zero_s)

        def zero_rows(lo, hi):
            def start(r, c2):
                pltpu.make_async_copy(zero_s, xb_ref.at[r], sem_pad).start()
                return c2

            def wait(r, c2):
                pltpu.make_async_copy(zero_s, xb_ref.at[r], sem_pad).wait()
                return c2

            lax.fori_loop(lo, hi, start, 0)
            lax.fori_loop(lo, hi, wait, 0)

        def per_expert(e, carry):
            zero_rows(first_ref[e] + cnt_ref[e], first_ref[e] + pad_ref[e])
            return carry

        lax.fori_loop(0, N_EXPERTS, per_expert, 0)
        zero_rows(first_ref[N_EXPERTS - 1] + pad_ref[N_EXPERTS - 1], xb_ref.shape[0])

    for kk in range(TOP_K):
        pltpu.make_async_copy(hn_ref, xb_ref.at[pl.ds(0, tm)], sem).wait()


def _dispatch(hn3, dest, counts, padded, pstart, n_rows, tm=256):
    M, s, lanes = hn3.shape
    kernel = functools.partial(_dispatch_kernel, tm=tm)
    grid_spec = pltpu.PrefetchScalarGridSpec(
        num_scalar_prefetch=3,
        grid=(M // tm,),
        in_specs=[pl.BlockSpec((1, 1, TOP_K * tm), lambda i, *_: (i, 0, 0)),
                  pl.BlockSpec((tm, s, lanes), lambda i, *_: (i, 0, 0))],
        out_specs=pl.BlockSpec(memory_space=pl.ANY),
        scratch_shapes=[pltpu.SMEM((1, TOP_K * tm), jnp.int32), pltpu.VMEM((s, lanes), F32),
                        pltpu.SemaphoreType.DMA(()), pltpu.SemaphoreType.DMA(()), pltpu.SemaphoreType.DMA(())],
    )
    return pl.pallas_call(
        kernel,
        grid_spec=grid_spec,
        out_shape=jax.ShapeDtypeStruct((n_rows, s, lanes), F32),
        compiler_params=_cparams(("arbitrary",)),
        name="dispatch",
    )(counts, padded, pstart, dest.reshape(M // tm, 1, TOP_K * tm), hn3)


def _expert_kernel(be_ref, nu_ref, x_ref, wu_ref, bu_ref, wd_ref, bdn_ref, y_ref, wu_s, wd_s, *, f):
    i = pl.program_id(0)
    e = be_ref[i]
    changed = jnp.logical_or(i == 0, e != be_ref[jnp.maximum(i - 1, 0)])

    @pl.when(changed)
    def _():
        wu_s[...] = wu_ref[0].astype(BF16)
        wd_s[...] = wd_ref[0].astype(BF16)

    @pl.when(i < nu_ref[0])
    def _():
        n_slab = x_ref.shape[1]
        x = jnp.concatenate([x_ref[:, s, :] for s in range(n_slab)], axis=1).astype(BF16)
        hcat = _dot(x, wu_s[...]) + bu_ref[0]
        gate = jnp.minimum(hcat[:, :f], SWIGLU_LIMIT)
        lin = jnp.clip(hcat[:, f:], -SWIGLU_LIMIT, SWIGLU_LIMIT)
        act = (lin + 1.0) * (gate * jax.nn.sigmoid(SWIGLU_ALPHA * gate))
        y = _dot(act.astype(BF16), wd_s[...]) + bdn_ref[0]
        for s in range(n_slab):
            y_ref[:, s, :] = y[:, s * LANES:(s + 1) * LANES]

    @pl.when(i >= nu_ref[0])
    def _():
        y_ref[...] = jnp.zeros_like(y_ref)


def _experts(x_buf, blk_expert, n_used, w_up, b_up, w_down, b_down, blk):
    P, s, lanes = x_buf.shape
    E, D, f2 = w_up.shape
    f = f2 // 2
    kernel = functools.partial(_expert_kernel, f=f)
    row_blk = lambda i, be, nu: (jnp.minimum(i, nu[0] - 1), 0, 0)
    grid_spec = pltpu.PrefetchScalarGridSpec(
        num_scalar_prefetch=2,
        grid=(P // blk,),
        in_specs=[
            pl.BlockSpec((blk, s, lanes), row_blk),
            pl.BlockSpec((1, D, f2), lambda i, be, nu: (be[i], 0, 0)),
            pl.BlockSpec((1, 1, f2), lambda i, be, nu: (be[i], 0, 0)),
            pl.BlockSpec((1, f, D), lambda i, be, nu: (be[i], 0, 0)),
            pl.BlockSpec((1, 1, D), lambda i, be, nu: (be[i], 0, 0)),
        ],
        out_specs=pl.BlockSpec((blk, s, lanes), lambda i, be, nu: (i, 0, 0)),
        scratch_shapes=[pltpu.VMEM((D, f2), BF16), pltpu.VMEM((f, D), BF16)],
    )
    return pl.pallas_call(
        kernel,
        grid_spec=grid_spec,
        out_shape=jax.ShapeDtypeStruct((P, s, lanes), F32),
        compiler_params=_cparams(("arbitrary",)),
        name="experts",
    )(blk_expert, n_used, x_buf, w_up, b_up.reshape(E, 1, f2), w_down, b_down.reshape(E, 1, D))


def _combine_kernel(dest_ref, h_ref, g_ref, fw_ref, yb_ref, o_ref, dest_smem, yg_s, sem_idx, sem, *, tm):
    cp = pltpu.make_async_copy(dest_ref.at[0], dest_smem, sem_idx)
    cp.start()
    cp.wait()

    def issue(t, carry):
        for kk in range(TOP_K):
            pltpu.make_async_copy(yb_ref.at[dest_smem[0, t * TOP_K + kk]], yg_s.at[kk, t], sem).start()
        return carry

    lax.fori_loop(0, tm, issue, 0)
    g = g_ref[...]
    h = h_ref[...]
    for kk in range(TOP_K):
        pltpu.make_async_copy(yb_ref.at[pl.ds(0, tm)], yg_s.at[kk], sem).wait()
    pieces = []
    for s in range(yg_s.shape[2]):
        acc = h[:, s * LANES:(s + 1) * LANES]
        for kk in range(TOP_K):
            acc = acc + g[:, kk:kk + 1] * yg_s[kk, :, s, :]
        pieces.append(acc)
    acc = jnp.concatenate(pieces, axis=1)
    ms = jnp.mean(acc * acc, axis=-1, keepdims=True)
    o_ref[...] = acc * lax.rsqrt(ms + RMS_EPS) * fw_ref[...]


def _combine(h, y_buf, dest, gates, final_w, tm=256):
    M, D = h.shape
    _, s, lanes = y_buf.shape
    kernel = functools.partial(_combine_kernel, tm=tm)
    return pl.pallas_call(
        kernel,
        grid=(M // tm,),
        in_specs=[
            pl.BlockSpec((1, 1, TOP_K * tm), lambda i: (i, 0, 0)),
            pl.BlockSpec((tm, D), lambda i: (i, 0)),
            pl.BlockSpec((tm, LANES), lambda i: (i, 0)),
            pl.BlockSpec((1, D), lambda i: (0, 0)),
            pl.BlockSpec(memory_space=pl.ANY),
        ],
        out_specs=pl.BlockSpec((tm, D), lambda i: (i, 0)),
        out_shape=jax.ShapeDtypeStruct((M, D), F32),
        scratch_shapes=[pltpu.SMEM((1, TOP_K * tm), jnp.int32), pltpu.VMEM((TOP_K, tm, s, lanes), F32),
                        pltpu.SemaphoreType.DMA(()), pltpu.SemaphoreType.DMA(())],
        compiler_params=_cparams(("arbitrary",)),
        name="combine",
    )(dest.reshape(M // tm, 1, TOP_K * tm), h, gates, final_w.reshape(1, D).astype(F32), y_buf)


def kernel(x, norm1_w, w_in, attn_rpb, attn_norm_w, rwkv_mu_prev, rwkv_mu_next, rwkv_w0, rwkv_w2, rwkv_a0, rwkv_a2, rwkv_g2, rwkv_k_k, rwkv_k_a, rwkv_r_k, rwkv_ln_w, rwkv_ln_b, w_o, norm2_w, router_w, router_b, expert_w_up, expert_b_up, expert_w_down, expert_b_down, final_norm_w):
    B, T, D = x.shape
    M = B * T
    depth = norm1_w.shape[0]
    assert depth == 1, "the combine stage folds in the final norm, so it must follow the only layer"
    c = rwkv_k_k.shape[-1]
    att_w = attn_norm_w.shape[-1]
    blk = 256
    h = x.reshape(M, D)
    for l in range(depth):
        qkv, p = _in_proj(h, norm1_w[l], w_in[l].astype(BF16), 3 * att_w)
        att = _na2d(qkv.reshape(B, T, 3 * att_w), _na_bias_table(attn_rpb[l]), att_w)
        kq, rq, kd, bk, egl, v, bonus, gate = _rwkv_prep(
            p.reshape(B, T, -1), rwkv_mu_prev[l], rwkv_mu_next[l], rwkv_w0[l].reshape(-1), rwkv_w2[l],
            rwkv_a0[l].reshape(-1), rwkv_a2[l], rwkv_g2[l], rwkv_k_k[l], rwkv_k_a[l], rwkv_r_k[l].reshape(-1))
        y = _rwkv_scan(kq, rq, kd, bk, v, egl)
        h, hn3, idx_pad, gates_pad, rank_pad, counts = _mix(
            h, att.reshape(M, att_w), y.reshape(2, M, c), bonus.reshape(M, c), gate.reshape(M, c),
            attn_norm_w[l], rwkv_ln_w[l], rwkv_ln_b[l], w_o[l], norm2_w[l], router_w[l], router_b[l])
        dest, cnt, padded, pstart, blk_expert, n_used = _route(
            idx_pad[:, :TOP_K], rank_pad[:, :TOP_K], counts[0, :N_EXPERTS], blk)
        n_rows = (M * TOP_K + N_EXPERTS * blk) // blk * blk
        x_buf = _dispatch(hn3, dest, cnt, padded, pstart, n_rows)
        y_buf = _experts(x_buf, blk_expert, n_used, expert_w_up[l], expert_b_up[l], expert_w_down[l],
                         expert_b_down[l], blk)
        h = _combine(h, y_buf, dest, gates_pad, final_norm_w)
    return h.reshape(B, T, D)
```

```python
import functools

import numpy as np
import jax
import jax.numpy as jnp
from jax import lax
from jax.experimental import pallas as pl
from jax.experimental.pallas import tpu as pltpu

F32 = jnp.float32
BF16 = jnp.bfloat16

HEAD_DIM = 64
GRID_W = 64
WIN_H = 8
WIN_W = 16
N_EXPERTS = 32
TOP_K = 4
SWIGLU_ALPHA = 1.702
SWIGLU_LIMIT = 7.0
RMS_EPS = 1e-5
GN_EPS = 64e-5
D_LORA = 64
D_GATE_LORA = 128
CHUNK = 64
LANES = 128
MASK_NEG = -1e30
VMEM_LIMIT = 56 * 1024 * 1024


def _cparams(sem):
    return pltpu.CompilerParams(dimension_semantics=sem, vmem_limit_bytes=VMEM_LIMIT)


def _dot(a, b):
    return jnp.dot(a, b, preferred_element_type=F32)


def _split2(x):
    hi = x.astype(BF16)
    lo = (x - hi.astype(F32)).astype(BF16)
    return hi, lo


def _dot_exact_lhs(a01, x):
    hi, lo = _split2(x)
    return _dot(a01, hi) + _dot(a01, lo)


def _dot_exact_rhs(x, b01):
    hi, lo = _split2(x)
    return _dot(hi, b01) + _dot(lo, b01)


def _inproj_kernel(x_ref, nw_ref, w_ref, qkv_ref, p_ref):
    x = x_ref[...]
    ms = jnp.mean(x * x, axis=-1, keepdims=True)
    hn = (x * lax.rsqrt(ms + RMS_EPS) * nw_ref[...]).astype(BF16)
    n_qkv = qkv_ref.shape[-1]
    qkv_ref[...] = _dot(hn, w_ref[:, :n_qkv]).astype(BF16)
    p_ref[...] = _dot(hn, w_ref[:, n_qkv:]).astype(BF16)


def _in_proj(x2d, norm_w, w_in_bf16, n_qkv, tm=512):
    M, D = x2d.shape
    n_all = w_in_bf16.shape[1]
    n_p = n_all - n_qkv
    return pl.pallas_call(
        _inproj_kernel,
        grid=(M // tm,),
        in_specs=[
            pl.BlockSpec((tm, D), lambda i: (i, 0)),
            pl.BlockSpec((1, D), lambda i: (0, 0)),
            pl.BlockSpec((D, n_all), lambda i: (0, 0)),
        ],
        out_specs=[
            pl.BlockSpec((tm, n_qkv), lambda i: (i, 0)),
            pl.BlockSpec((tm, n_p), lambda i: (i, 0)),
        ],
        out_shape=[jax.ShapeDtypeStruct((M, n_qkv), BF16), jax.ShapeDtypeStruct((M, n_p), BF16)],
        compiler_params=_cparams(("parallel",)),
        name="in_proj",
    )(x2d, norm_w.reshape(1, D), w_in_bf16)


def _na_bias_table(rpb):
    H = rpb.shape[0]
    cols = np.arange(GRID_W)
    col_start = np.clip(cols - WIN_W // 2, 0, GRID_W - WIN_W)
    kc = np.arange(GRID_W)
    valid = (kc[None, :] >= col_start[:, None]) & (kc[None, :] < col_start[:, None] + WIN_W)
    dc = kc[None, :] - cols[:, None] + (WIN_W - 1)
    var = np.arange(WIN_H)
    wi = np.arange(WIN_H)
    dr = wi[None, :] - var[:, None] + (WIN_H - 1)
    row_sel = (dr[:, :, None] == np.arange(2 * WIN_H - 1)).astype(np.float32)
    col_sel = ((dc[:, :, None] == np.arange(2 * WIN_W - 1)) & valid[:, :, None]).astype(np.float32)
    tab = jnp.einsum("vir,hrd,xyd->vhxiy", row_sel, rpb.astype(F32), col_sel, precision=lax.Precision.HIGHEST)
    tab = jnp.where(valid[None, None, :, None, :], tab, MASK_NEG)
    return tab.reshape(WIN_H, H, GRID_W, WIN_H * GRID_W)


def _na_kernel(q_ref, k_ref, v_ref, bias_ref, o_ref, *, rows_per_step, n_rows):
    i = pl.program_id(2)
    lane = lax.broadcasted_iota(jnp.int32, (1, LANES), 1)
    head0 = lane < HEAD_DIM
    scale = HEAD_DIM ** -0.5
    nkeys = WIN_H * GRID_W

    def scores(rr):
        r = i * rows_per_step + rr
        r_start = jnp.clip(r - WIN_H // 2, 0, n_rows - WIN_H)
        var = r - r_start
        q = q_ref[0, rr * GRID_W:(rr + 1) * GRID_W, :]
        koff = pl.multiple_of(r_start * GRID_W, GRID_W)
        kwin = k_ref[0, pl.ds(koff, nkeys), :]
        out = []
        for h in range(2):
            qh = jnp.where(head0 if h == 0 else jnp.logical_not(head0), q, jnp.zeros_like(q))
            s = lax.dot_general(qh, kwin, (((1,), (1,)), ((), ())), preferred_element_type=F32)
            out.append(s * scale + bias_ref[var, h])
        return koff, out

    def finish(rr, koff, ss):
        vwin = v_ref[0, pl.ds(koff, nkeys), :]
        outs = []
        for s in ss:
            m = jnp.max(s, axis=-1, keepdims=True)
            e = jnp.exp(s - m)
            denom = jnp.sum(e, axis=-1, keepdims=True)
            outs.append(_dot(e.astype(BF16), vwin) / denom)
        o_ref[0, rr * GRID_W:(rr + 1) * GRID_W, :] = jnp.where(head0, outs[0], outs[1]).astype(o_ref.dtype)

    pending = scores(0)
    for rr in range(rows_per_step):
        nxt = scores(rr + 1) if rr + 1 < rows_per_step else None
        finish(rr, *pending)
        pending = nxt


def _na2d(qkv, bias_tab, att_w, rows_per_step=8):
    B, T, _ = qkv.shape
    n_rows = T // GRID_W
    assert n_rows >= WIN_H and n_rows % rows_per_step == 0
    n_pairs = att_w // LANES
    tq = rows_per_step * GRID_W
    kernel = functools.partial(_na_kernel, rows_per_step=rows_per_step, n_rows=n_rows)
    return pl.pallas_call(
        kernel,
        grid=(B, n_pairs, n_rows // rows_per_step),
        in_specs=[
            pl.BlockSpec((1, tq, LANES), lambda b, hp, i: (b, i, hp)),
            pl.BlockSpec((1, T, LANES), lambda b, hp, i: (b, 0, n_pairs + hp)),
            pl.BlockSpec((1, T, LANES), lambda b, hp, i: (b, 0, 2 * n_pairs + hp)),
            pl.BlockSpec((WIN_H, 2, GRID_W, WIN_H * GRID_W), lambda b, hp, i: (0, hp, 0, 0)),
        ],
        out_specs=pl.BlockSpec((1, tq, LANES), lambda b, hp, i: (b, i, hp)),
        out_shape=jax.ShapeDtypeStruct((B, T, att_w), BF16),
        compiler_params=_cparams(("parallel", "parallel", "arbitrary")),
        name="na2d",
    )(qkv, qkv, qkv, bias_tab)


def _prep_kernel(p_ref, pprev_ref, pnext_ref, mup_ref, mun_ref, w0_ref, w2_ref, a0_ref, a2_ref, g2_ref,
                 kk_ref, ka_ref, rk_ref, bd_ref, tri_ref, ones_ref,
                 kq_ref, rq_ref, kd_ref, bk_ref, egl_ref, v_ref, bonus_ref, gate_ref, *, tb, c):
    i = pl.program_id(1)
    nb = pl.num_programs(1)
    p = p_ref[0].astype(F32)
    halo = pprev_ref.shape[1]
    prow = jnp.where(i > 0, pprev_ref[0, halo - 1:halo, :].astype(F32), 0.0)
    nrow = jnp.where(i < nb - 1, pnext_ref[0, 0:1, :].astype(F32), 0.0)
    ridx = lax.broadcasted_iota(jnp.int32, (tb, 1), 0)
    p_prev = jnp.where(ridx == 0, prow, pltpu.roll(p, 1, 0))
    p_next = jnp.where(ridx == tb - 1, nrow, pltpu.roll(p, tb - 1, 0))
    pm = p + mup_ref[...] * (p_prev - p) + mun_ref[...] * (p_next - p)
    r = pm[:, :c]
    k = pm[:, c:2 * c]
    v = pm[:, 2 * c:3 * c]
    o = 3 * c
    dw = pm[:, o:o + 2 * D_LORA]
    da = pm[:, o + 2 * D_LORA:o + 4 * D_LORA]
    dg = pm[:, o + 4 * D_LORA:o + 4 * D_LORA + D_GATE_LORA]

    wl = w0_ref[...] + _dot(jnp.tanh(dw).astype(BF16), w2_ref[...])
    neg = -wl
    softplus = jnp.maximum(neg, 0.0) + jnp.log(1.0 + jnp.exp(-jnp.abs(neg)))
    logw = -jnp.exp(-softplus - 0.5)
    a = jax.nn.sigmoid(a0_ref[...] + _dot(da.astype(BF16), a2_ref[...]))
    gate = _dot(jax.nn.sigmoid(dg).astype(BF16), g2_ref[...])

    bd = bd_ref[...]
    kkr = k * kk_ref[...]
    ssq = _dot_exact_rhs(kkr * kkr, bd)
    kk = kkr / jnp.maximum(jnp.sqrt(ssq), 1e-12)

    ksum = jnp.zeros_like(k)
    nchunk = tb // CHUNK
    for z in range(2):
        a_z = a[:, z * c:(z + 1) * c]
        lw = logw[:, z * c:(z + 1) * c]
        kdir = k * (1.0 + (a_z - 1.0) * ka_ref[...])
        ksum = ksum + kdir
        b_z = a_z * kk
        g_incl = _dot_exact_lhs(tri_ref[z], lw)
        g_all = _dot_exact_lhs(ones_ref[...], lw)
        e_neg = jnp.exp(-g_incl)
        kq_ref[z, 0] = (kk * jnp.exp(g_incl - lw)).astype(BF16)
        rq_ref[z, 0] = (r * jnp.exp(g_incl)).astype(BF16)
        kd_ref[z, 0] = (kdir * e_neg).astype(BF16)
        bk_ref[z, 0] = (b_z * e_neg).astype(BF16)
        egl_ref[z, 0] = jnp.exp(g_all).reshape(nchunk, CHUNK, c)[:, 0:1, :]
    coef = _dot_exact_rhs(r * ksum * rk_ref[...], bd)
    v_ref[0] = v.astype(BF16)
    bonus_ref[0] = (coef * v).astype(BF16)
    gate_ref[0] = gate.astype(BF16)


def _rwkv_prep(p, mu_prev, mu_next, w0, w2, a0, a2, g2, k_k, k_a, r_k, tb=256):
    B, T, pw = p.shape
    c = k_k.shape[0]
    assert T % tb == 0 and tb % CHUNK == 0
    halo = 16
    nhb = T // halo
    nchunk = tb // CHUNK
    w2bd = jnp.zeros((2 * D_LORA, 2 * c), F32)
    a2bd = jnp.zeros((2 * D_LORA, 2 * c), F32)
    for z in range(2):
        w2bd = w2bd.at[z * D_LORA:(z + 1) * D_LORA, z * c:(z + 1) * c].set(w2[z])
        a2bd = a2bd.at[z * D_LORA:(z + 1) * D_LORA, z * c:(z + 1) * c].set(a2[z])
    ch = np.arange(c) // HEAD_DIM
    bd = jnp.asarray((ch[:, None] == ch[None, :]).astype(np.float32), BF16)
    t = np.arange(tb)
    same = (t[:, None] // CHUNK) == (t[None, :] // CHUNK)
    tri = np.stack([same & (t[None, :] <= t[:, None]), same & (t[None, :] >= t[:, None])]).astype(np.float32)
    tri = jnp.asarray(tri, BF16)
    ones = jnp.asarray(same.astype(np.float32), BF16)
    row = lambda u: u.reshape(1, -1).astype(F32)
    const = lambda shape: pl.BlockSpec(shape, lambda b, i: (0,) * len(shape))
    dir_out = pl.BlockSpec((2, 1, tb, c), lambda b, i: (0, b, i, 0))
    tok_out = pl.BlockSpec((1, tb, c), lambda b, i: (b, i, 0))
    kernel = functools.partial(_prep_kernel, tb=tb, c=c)
    return pl.pallas_call(
        kernel,
        grid=(B, T // tb),
        in_specs=[
            pl.BlockSpec((1, tb, pw), lambda b, i: (b, i, 0)),
            pl.BlockSpec((1, halo, pw), lambda b, i: (b, jnp.maximum(i * (tb // halo) - 1, 0), 0)),
            pl.BlockSpec((1, halo, pw), lambda b, i: (b, jnp.minimum((i + 1) * (tb // halo), nhb - 1), 0)),
            const((1, pw)), const((1, pw)),
            const((1, 2 * c)), const((2 * D_LORA, 2 * c)),
            const((1, 2 * c)), const((2 * D_LORA, 2 * c)),
            const((D_GATE_LORA, c)),
            const((1, c)), const((1, c)), const((1, c)),
            const((c, c)), const((2, tb, tb)), const((tb, tb)),
        ],
        out_specs=[dir_out, dir_out, dir_out, dir_out,
                   pl.BlockSpec((2, 1, nchunk, 1, c), lambda b, i: (0, b, i, 0, 0)),
                   tok_out, tok_out, tok_out],
        out_shape=[jax.ShapeDtypeStruct((2, B, T, c), BF16)] * 4
        + [jax.ShapeDtypeStruct((2, B, T // CHUNK, 1, c), F32)]
        + [jax.ShapeDtypeStruct((B, T, c), BF16)] * 3,
        compiler_params=_cparams(("parallel", "parallel")),
        name="rwkv_prep",
    )(p, p, p, row(mu_prev), row(mu_next), row(w0), w2bd.astype(BF16), row(a0), a2bd.astype(BF16),
      g2.astype(BF16), row(k_k), row(k_a), row(r_k), bd, tri, ones)


def _bmm(a, b):
    return jnp.einsum("cij,cjk->cik", a.astype(BF16), b.astype(BF16), preferred_element_type=F32)


def _bmm_nt(a, b):
    return jnp.einsum("cik,cjk->cij", a.astype(BF16), b.astype(BF16), preferred_element_type=F32)


def _scan_kernel(*refs, cg, n_pairs):
    in_refs = (refs[:6], refs[6:12])
    y_refs = refs[12:14]
    h_s, mc_s, cc_s, hs_s = refs[14:]
    gi = pl.program_id(2)

    @pl.when(gi == 0)
    def _():
        h_s[...] = jnp.zeros_like(h_s)

    two = 2 * CHUNK
    lane = lax.broadcasted_iota(jnp.int32, (1, 1, LANES), 2)
    head0 = lane < HEAD_DIM
    row = lax.broadcasted_iota(jnp.int32, (1, two, two), 1)
    col = lax.broadcasted_iota(jnp.int32, (1, two, two), 2)
    same = (row // CHUNK) == (col // CHUNK)
    ahead = (row % CHUNK) - (col % CHUNK)
    eye = (row == col).astype(F32)
    chains = [(z, pp) for z in range(2) for pp in range(n_pairs)]

    def stacked(ref, pp):
        x = ref[0, :, pp * LANES:(pp + 1) * LANES].reshape(cg, CHUNK, LANES)
        zero = jnp.zeros_like(x)
        return jnp.concatenate([jnp.where(head0, x, zero), jnp.where(head0, zero, x)], axis=1)

    ops = []
    for z, pp in chains:
        kq_ref, rq_ref, kd_ref, bk_ref, v_ref, egl_ref = in_refs[z]
        ops.append(dict(
            kq=stacked(kq_ref.at[0], pp), rq=stacked(rq_ref.at[0], pp), kd=stacked(kd_ref.at[0], pp),
            bk=stacked(bk_ref.at[0], pp), v=stacked(v_ref, pp),
            egl=egl_ref[0, 0, :, :, pp * LANES:(pp + 1) * LANES],
            strict=same & ((ahead > 0) if z == 0 else (ahead < 0)),
            incl=same & ((ahead >= 0) if z == 0 else (ahead <= 0))))

    for o in ops:
        gram = _bmm_nt(jnp.concatenate([o["kq"], o["rq"]], axis=1), jnp.concatenate([o["kd"], o["bk"]], axis=1))
        o["a_kk"] = jnp.where(o["strict"], gram[:, :two, :two], 0.0)
        o["pw"] = -jnp.where(o["strict"], gram[:, :two, two:], 0.0)
        o["a_rk"] = jnp.where(o["incl"], gram[:, two:, :two], 0.0)
        o["a_rb"] = jnp.where(o["incl"], gram[:, two:, two:], 0.0)
        o["tinv"] = eye + o["pw"]

    for o in ops:
        o["pw"] = _bmm(o["pw"], o["pw"])
    for _ in range(4):
        for o in ops:
            both = _bmm(o["pw"], jnp.concatenate([o["pw"].astype(BF16), o["tinv"].astype(BF16)], axis=2))
            o["pw"] = both[:, :, :two]
            o["tinv"] = o["tinv"] + both[:, :, two:]
    for o in ops:
        o["tinv"] = o["tinv"] + _bmm(o["pw"], o["tinv"])
    for o in ops:
        o["akv"] = _bmm(o["a_kk"], o["v"])
    for o in ops:
        o["wu"] = _bmm(o["tinv"], jnp.concatenate([o["kq"], o["akv"].astype(BF16)], axis=2))
    for n, o in enumerate(ops):
        bkg_t = jnp.swapaxes(o["bk"].astype(F32) * o["egl"], 1, 2)
        kdg_t = jnp.swapaxes(o["kd"].astype(F32) * o["egl"], 1, 2)
        bwu = _bmm(bkg_t, o["wu"])
        mc_s[n] = (eye * o["egl"] - bwu[:, :, :LANES]).astype(BF16)
        cc_s[n] = _bmm(kdg_t, o["v"]) - bwu[:, :, LANES:]
    for o in ops:
        rwu = _bmm(o["a_rb"], o["wu"])
        o["q"] = o["rq"].astype(F32) - rwu[:, :, :LANES]
        o["y0"] = _bmm(o["a_rk"], o["v"]) - rwu[:, :, LANES:]

    def body(ci, carry):
        for n, (z, _) in enumerate(chains):
            cidx = ci if z == 0 else cg - 1 - ci
            hb = h_s[n].astype(BF16)
            hs_s[n, cidx] = hb
            h_s[n] = _dot(mc_s[n, cidx], hb) + cc_s[n, cidx]
        return carry

    lax.fori_loop(0, cg, body, 0)
    for n, (z, pp) in enumerate(chains):
        y = _bmm(ops[n]["q"], hs_s[n]) + ops[n]["y0"]
        y = (y[:, :CHUNK, :] + y[:, CHUNK:, :]).reshape(cg * CHUNK, LANES)
        y_refs[z][0, :, pp * LANES:(pp + 1) * LANES] = y.astype(y_refs[z].dtype)


def _rwkv_scan(kq, rq, kd, bk, v, egl, cg=8, n_pairs=2):
    _, B, T, c = kq.shape
    lw = n_pairs * LANES
    tl = cg * CHUNK
    ng = T // tl
    assert T % tl == 0 and c % lw == 0
    n_chain = 2 * n_pairs

    in_specs, args = [], []
    for z in range(2):
        tmap = (lambda g: g) if z == 0 else (lambda g: ng - 1 - g)
        dir_in = pl.BlockSpec((1, 1, tl, lw), lambda b, hp, g, z=z, tmap=tmap: (z, b, tmap(g), hp))
        in_specs += [dir_in] * 4
        in_specs.append(pl.BlockSpec((1, tl, lw), lambda b, hp, g, tmap=tmap: (b, tmap(g), hp)))
        in_specs.append(pl.BlockSpec((1, 1, cg, 1, lw), lambda b, hp, g, z=z, tmap=tmap: (z, b, tmap(g), 0, hp)))
        args += [kq, rq, kd, bk, v, egl]
    out_specs = [pl.BlockSpec((1, tl, lw), lambda b, hp, g: (b, g, hp)),
                 pl.BlockSpec((1, tl, lw), lambda b, hp, g: (b, ng - 1 - g, hp))]
    kernel = functools.partial(_scan_kernel, cg=cg, n_pairs=n_pairs)
    return pl.pallas_call(
        kernel,
        grid=(B, c // lw, ng),
        in_specs=in_specs,
        out_specs=out_specs,
        out_shape=[jax.ShapeDtypeStruct((B, T, c), BF16)] * 2,
        scratch_shapes=[pltpu.VMEM((n_chain, LANES, LANES), F32),
                        pltpu.VMEM((n_chain, cg, LANES, LANES), BF16),
                        pltpu.VMEM((n_chain, cg, LANES, LANES), F32),
                        pltpu.VMEM((n_chain, cg, LANES, LANES), BF16)],
        compiler_params=_cparams(("parallel", "parallel", "arbitrary")),
        name="rwkv_scan",
    )(*args)


def _mix_kernel(x_ref, att_ref, yf_ref, yb_ref, bonus_ref, gate_ref, anw_ref, lnw_ref, lnb_ref, bd_ref, wo_ref,
                n2w_ref, rw_ref, rb_ref, tril_ref, h_ref, hn_ref, idx_ref, gates_ref, rank_ref, cnt_ref, cnt_s):
    @pl.when(pl.program_id(0) == 0)
    def _():
        cnt_s[...] = jnp.zeros_like(cnt_s)

    att = att_ref[...].astype(F32)
    ms = jnp.mean(att * att, axis=-1, keepdims=True)
    att_n = att * lax.rsqrt(ms + RMS_EPS) * anw_ref[...]

    y = yf_ref[...].astype(F32) + yb_ref[...].astype(F32)
    bd = bd_ref[...]
    inv_n = 1.0 / HEAD_DIM
    mean = _dot_exact_rhs(y, bd) * inv_n
    yc = y - mean
    var = _dot_exact_rhs(yc * yc, bd) * inv_n
    yn = yc * lax.rsqrt(var + GN_EPS) * lnw_ref[...] + lnb_ref[...]
    rk = (yn + bonus_ref[...].astype(F32)) * gate_ref[...].astype(F32)

    att_w = att.shape[-1]
    mix = _dot(att_n.astype(BF16), wo_ref[:att_w, :]) + _dot(rk.astype(BF16), wo_ref[att_w:, :])
    h = x_ref[...] + mix
    h_ref[...] = h
    ms2 = jnp.mean(h * h, axis=-1, keepdims=True)
    hn = h * lax.rsqrt(ms2 + RMS_EPS) * n2w_ref[...]
    hn_ref[...] = hn

    hn_hi, hn_lo = _split2(hn)
    logits = _dot(hn_hi, rw_ref[0]) + _dot(hn_lo, rw_ref[0]) + _dot(hn_hi, rw_ref[1]) + rb_ref[...]
    lane = lax.broadcasted_iota(jnp.int32, logits.shape, 1)
    cur = logits
    vals, idxs = [], []
    for _ in range(TOP_K):
        m = jnp.max(cur, axis=-1, keepdims=True)
        sel = jnp.min(jnp.where(cur == m, lane, LANES), axis=-1, keepdims=True)
        vals.append(m)
        idxs.append(sel)
        cur = jnp.where(lane == sel, -jnp.inf, cur)
    es = [jnp.exp(vk - vals[0]) for vk in vals]
    tot = es[0] + es[1] + es[2] + es[3]
    onehot = jnp.zeros(logits.shape, F32)
    for kk in range(TOP_K):
        onehot = onehot + (lane == idxs[kk]).astype(F32)
    before = cnt_s[...] + _dot(tril_ref[...], onehot.astype(BF16))
    cnt_s[...] = cnt_s[...] + jnp.sum(onehot, axis=0, keepdims=True)
    cnt_ref[...] = cnt_s[...]

    idx_out = jnp.zeros(logits.shape, jnp.int32)
    g_out = jnp.zeros(logits.shape, F32)
    rank_out = jnp.zeros(logits.shape, F32)
    for kk in range(TOP_K):
        idx_out = jnp.where(lane == kk, idxs[kk], idx_out)
        g_out = jnp.where(lane == kk, es[kk] / tot, g_out)
        rank_k = jnp.sum(jnp.where(lane == idxs[kk], before, 0.0), axis=-1, keepdims=True)
        rank_out = jnp.where(lane == kk, rank_k, rank_out)
    idx_ref[...] = idx_out
    gates_ref[...] = g_out
    rank_ref[...] = rank_out.astype(jnp.int32)


def _mix(x2d, att, y_fwd, y_bwd, bonus, gate, attn_norm_w, ln_w, ln_b, w_o, norm2_w, router_w, router_b, tm=256):
    M, D = x2d.shape
    c = y_fwd.shape[-1]
    att_w = att.shape[-1]
    ch = np.arange(c) // HEAD_DIM
    bd = jnp.asarray((ch[:, None] == ch[None, :]).astype(np.float32), BF16)
    t = np.arange(tm)
    tril = jnp.asarray((t[None, :] < t[:, None]).astype(np.float32), BF16)
    rw = jnp.zeros((D, LANES), F32).at[:, :N_EXPERTS].set(router_w)
    rw = jnp.stack(_split2(rw))
    rb = jnp.full((1, LANES), -jnp.inf, F32).at[0, :N_EXPERTS].set(router_b)
    row = lambda u: u.reshape(1, -1).astype(F32)
    const = lambda shape: pl.BlockSpec(shape, lambda i: (0,) * len(shape))
    tok = lambda w: pl.BlockSpec((tm, w), lambda i: (i, 0))
    return pl.pallas_call(
        _mix_kernel,
        grid=(M // tm,),
        in_specs=[
            tok(D), tok(att_w), tok(c), tok(c), tok(c), tok(c),
            const((1, att_w)), const((1, c)), const((1, c)), const((c, c)), const((att_w + c, D)),
            const((1, D)), const((2, D, LANES)), const((1, LANES)), const((tm, tm)),
        ],
        out_specs=[
            tok(D), tok(D),
            tok(LANES), tok(LANES), tok(LANES),
            const((1, LANES)),
        ],
        out_shape=[jax.ShapeDtypeStruct((M, D), F32), jax.ShapeDtypeStruct((M, D), F32),
                   jax.ShapeDtypeStruct((M, LANES), jnp.int32), jax.ShapeDtypeStruct((M, LANES), F32),
                   jax.ShapeDtypeStruct((M, LANES), jnp.int32), jax.ShapeDtypeStruct((1, LANES), F32)],
        scratch_shapes=[pltpu.VMEM((1, LANES), F32)],
        compiler_params=_cparams(("arbitrary",)),
        name="mix_router",
    )(x2d, att, y_fwd, y_bwd, bonus, gate, row(attn_norm_w), row(ln_w), row(ln_b), bd, w_o.astype(BF16),
      row(norm2_w), rw, rb, tril)


def _route(top_idx, rank, counts, blk):
    M = top_idx.shape[0]
    counts = counts.astype(jnp.int32)
    padded = (counts + blk - 1) // blk * blk
    pend = jnp.cumsum(padded)
    pstart = pend - padded
    experts = jnp.arange(N_EXPERTS, dtype=jnp.int32)
    first = jnp.sum(jnp.where(top_idx[:, :, None] == experts, pstart, 0), axis=-1)
    dest = (first + rank).astype(jnp.int32)
    n_blk = (M * TOP_K + N_EXPERTS * blk) // blk
    starts = jnp.arange(n_blk, dtype=jnp.int32) * blk
    blk_expert = jnp.sum((pend[None, :] <= starts[:, None]).astype(jnp.int32), axis=1)
    blk_expert = jnp.minimum(blk_expert, N_EXPERTS - 1).astype(jnp.int32)
    n_used = (pend[-1] // blk).reshape(1).astype(jnp.int32)
    return dest, counts, padded.astype(jnp.int32), pstart.astype(jnp.int32), blk_expert, n_used


def _dispatch_kernel(cnt_ref, pad_ref, first_ref, dest_ref, hn_ref, xb_ref, dest_smem, zero_s, sem_idx, sem, sem_pad,
                     *, tm):
    i = pl.program_id(0)
    cp = pltpu.make_async_copy(dest_ref.at[0], dest_smem, sem_idx)
    cp.start()
    cp.wait()

    def issue(t, carry):
        for kk in range(TOP_K):
            row = dest_smem[0, t * TOP_K + kk]
            pltpu.make_async_copy(hn_ref.at[pl.ds(t, 1)], xb_ref.at[pl.ds(row, 1)], sem).start()
        return carry

    lax.fori_loop(0, tm, issue, 0)

    @pl.when(i == 0)
    def _():
        zero_s[...] = jnp.zeros_like(zero_s)

        def zero_rows(lo, hi):
            def start(r, c2):
                pltpu.make_async_copy(zero_s.at[pl.ds(0, 1)], xb_ref.at[pl.ds(r, 1)], sem_pad).start()
                return c2

            def wait(r, c2):
                pltpu.make_async_copy(zero_s.at[pl.ds(0, 1)], xb_ref.at[pl.ds(r, 1)], sem_pad).wait()
                return c2

            lax.fori_loop(lo, hi, start, 0)
            lax.fori_loop(lo, hi, wait, 0)

        def per_expert(e, carry):
            zero_rows(first_ref[e] + cnt_ref[e], first_ref[e] + pad_ref[e])
            return carry

        lax.fori_loop(0, N_EXPERTS, per_expert, 0)
        zero_rows(first_ref[N_EXPERTS - 1] + pad_ref[N_EXPERTS - 1], xb_ref.shape[0])

    for kk in range(TOP_K):
        pltpu.make_async_copy(hn_ref, xb_ref.at[pl.ds(0, tm)], sem).wait()


def _dispatch(hn, dest, counts, padded, pstart, n_rows, tm=256):
    M, D = hn.shape
    kernel = functools.partial(_dispatch_kernel, tm=tm)
    grid_spec = pltpu.PrefetchScalarGridSpec(
        num_scalar_prefetch=3,
        grid=(M // tm,),
        in_specs=[pl.BlockSpec((1, 1, TOP_K * tm), lambda i, *_: (i, 0, 0)),
                  pl.BlockSpec((tm, D), lambda i, *_: (i, 0))],
        out_specs=pl.BlockSpec(memory_space=pl.ANY),
        scratch_shapes=[pltpu.SMEM((1, TOP_K * tm), jnp.int32), pltpu.VMEM((8, D), F32),
                        pltpu.SemaphoreType.DMA(()), pltpu.SemaphoreType.DMA(()), pltpu.SemaphoreType.DMA(())],
    )
    return pl.pallas_call(
        kernel,
        grid_spec=grid_spec,
        out_shape=jax.ShapeDtypeStruct((n_rows, D), F32),
        compiler_params=_cparams(("arbitrary",)),
        name="dispatch",
    )(counts, padded, pstart, dest.reshape(M // tm, 1, TOP_K * tm), hn)


def _expert_kernel(be_ref, nu_ref, x_ref, wu_ref, bu_ref, wd_ref, bdn_ref, y_ref, wu_s, wd_s, *, f, n_split):
    i = pl.program_id(0)
    e = be_ref[i]
    changed = jnp.logical_or(i == 0, e != be_ref[jnp.maximum(i - 1, 0)])

    @pl.when(changed)
    def _():
        wu_s[...] = wu_ref[0].astype(BF16)
        wd_s[...] = wd_ref[0].astype(BF16)

    @pl.when(i < nu_ref[0])
    def _():
        rows = x_ref.shape[0] // n_split
        hcats = [_dot(x_ref[j * rows:(j + 1) * rows, :].astype(BF16), wu_s[...]) + bu_ref[0] for j in range(n_split)]
        for j, hcat in enumerate(hcats):
            gate = jnp.minimum(hcat[:, :f], SWIGLU_LIMIT)
            lin = jnp.clip(hcat[:, f:], -SWIGLU_LIMIT, SWIGLU_LIMIT)
            act = (lin + 1.0) * (gate * jax.nn.sigmoid(SWIGLU_ALPHA * gate))
            y_ref[j * rows:(j + 1) * rows, :] = _dot(act.astype(BF16), wd_s[...]) + bdn_ref[0]

    @pl.when(i >= nu_ref[0])
    def _():
        y_ref[...] = jnp.zeros_like(y_ref)


def _experts(x_buf, blk_expert, n_used, w_up, b_up, w_down, b_down, blk, n_split=2):
    P, D = x_buf.shape
    E, _, f2 = w_up.shape
    f = f2 // 2
    kernel = functools.partial(_expert_kernel, f=f, n_split=n_split)
    grid_spec = pltpu.PrefetchScalarGridSpec(
        num_scalar_prefetch=2,
        grid=(P // blk,),
        in_specs=[
            pl.BlockSpec((blk, D), lambda i, be, nu: (jnp.minimum(i, nu[0] - 1), 0)),
            pl.BlockSpec((1, D, f2), lambda i, be, nu: (be[i], 0, 0)),
            pl.BlockSpec((1, 1, f2), lambda i, be, nu: (be[i], 0, 0)),
            pl.BlockSpec((1, f, D), lambda i, be, nu: (be[i], 0, 0)),
            pl.BlockSpec((1, 1, D), lambda i, be, nu: (be[i], 0, 0)),
        ],
        out_specs=pl.BlockSpec((blk, D), lambda i, be, nu: (i, 0)),
        scratch_shapes=[pltpu.VMEM((D, f2), BF16), pltpu.VMEM((f, D), BF16)],
    )
    return pl.pallas_call(
        kernel,
        grid_spec=grid_spec,
        out_shape=jax.ShapeDtypeStruct((P, D), F32),
        compiler_params=_cparams(("arbitrary",)),
        name="experts",
    )(blk_expert, n_used, x_buf, w_up, b_up.reshape(E, 1, f2), w_down, b_down.reshape(E, 1, D))


def _combine_kernel(dest_ref, h_ref, g_ref, fw_ref, yb_ref, o_ref, dest_smem, yg_s, sem_idx, sem, *, tm):
    cp = pltpu.make_async_copy(dest_ref.at[0], dest_smem, sem_idx)
    cp.start()
    cp.wait()

    def issue(t, carry):
        for kk in range(TOP_K):
            row = dest_smem[0, t * TOP_K + kk]
            pltpu.make_async_copy(yb_ref.at[pl.ds(row, 1)], yg_s.at[kk, pl.ds(t, 1)], sem).start()
        return carry

    lax.fori_loop(0, tm, issue, 0)
    g = g_ref[...]
    acc = h_ref[...]
    for kk in range(TOP_K):
        pltpu.make_async_copy(yb_ref.at[pl.ds(0, tm)], yg_s.at[kk], sem).wait()
    for kk in range(TOP_K):
        acc = acc + g[:, kk:kk + 1] * yg_s[kk]
    ms = jnp.mean(acc * acc, axis=-1, keepdims=True)
    o_ref[...] = acc * lax.rsqrt(ms + RMS_EPS) * fw_ref[...]


def _combine(h, y_buf, dest, gates, final_w, tm=256):
    M, D = h.shape
    kernel = functools.partial(_combine_kernel, tm=tm)
    return pl.pallas_call(
        kernel,
        grid=(M // tm,),
        in_specs=[
            pl.BlockSpec((1, 1, TOP_K * tm), lambda i: (i, 0, 0)),
            pl.BlockSpec((tm, D), lambda i: (i, 0)),
            pl.BlockSpec((tm, LANES), lambda i: (i, 0)),
            pl.BlockSpec((1, D), lambda i: (0, 0)),
            pl.BlockSpec(memory_space=pl.ANY),
        ],
        out_specs=pl.BlockSpec((tm, D), lambda i: (i, 0)),
        out_shape=jax.ShapeDtypeStruct((M, D), F32),
        scratch_shapes=[pltpu.SMEM((1, TOP_K * tm), jnp.int32), pltpu.VMEM((TOP_K, tm, D), F32),
                        pltpu.SemaphoreType.DMA(()), pltpu.SemaphoreType.DMA(())],
        compiler_params=_cparams(("arbitrary",)),
        name="combine",
    )(dest.reshape(M // tm, 1, TOP_K * tm), h, gates, final_w.reshape(1, D).astype(F32), y_buf)


def kernel(x, norm1_w, w_in, attn_rpb, attn_norm_w, rwkv_mu_prev, rwkv_mu_next, rwkv_w0, rwkv_w2, rwkv_a0, rwkv_a2, rwkv_g2, rwkv_k_k, rwkv_k_a, rwkv_r_k, rwkv_ln_w, rwkv_ln_b, w_o, norm2_w, router_w, router_b, expert_w_up, expert_b_up, expert_w_down, expert_b_down, final_norm_w):
    B, T, D = x.shape
    M = B * T
    depth = norm1_w.shape[0]
    assert depth == 1, "the combine stage folds in the final norm, so it must follow the only layer"
    c = rwkv_k_k.shape[-1]
    att_w = attn_norm_w.shape[-1]
    blk = 256
    h = x.reshape(M, D)
    for l in range(depth):
        qkv, p = _in_proj(h, norm1_w[l], w_in[l].astype(BF16), 3 * att_w)
        att = _na2d(qkv.reshape(B, T, 3 * att_w), _na_bias_table(attn_rpb[l]), att_w)
        kq, rq, kd, bk, egl, v, bonus, gate = _rwkv_prep(
            p.reshape(B, T, -1), rwkv_mu_prev[l], rwkv_mu_next[l], rwkv_w0[l].reshape(-1), rwkv_w2[l],
            rwkv_a0[l].reshape(-1), rwkv_a2[l], rwkv_g2[l], rwkv_k_k[l], rwkv_k_a[l], rwkv_r_k[l].reshape(-1))
        y_fwd, y_bwd = _rwkv_scan(kq, rq, kd, bk, v, egl)
        h, hn, idx_pad, gates_pad, rank_pad, counts = _mix(
            h, att.reshape(M, att_w), y_fwd.reshape(M, c), y_bwd.reshape(M, c), bonus.reshape(M, c),
            gate.reshape(M, c),
            attn_norm_w[l], rwkv_ln_w[l], rwkv_ln_b[l], w_o[l], norm2_w[l], router_w[l], router_b[l])
        dest, cnt, padded, pstart, blk_expert, n_used = _route(
            idx_pad[:, :TOP_K], rank_pad[:, :TOP_K], counts[0, :N_EXPERTS], blk)
        n_rows = (M * TOP_K + N_EXPERTS * blk) // blk * blk
        x_buf = _dispatch(hn, dest, cnt, padded, pstart, n_rows)
        y_buf = _experts(x_buf, blk_expert, n_used, expert_w_up[l], expert_b_up[l], expert_w_down[l],
                         expert_b_down[l], blk)
        h = _combine(h, y_buf, dest, gates_pad, final_norm_w)
    return h.reshape(B, T, D)
```

```python
import functools

import numpy as np
import jax
import jax.numpy as jnp
from jax import lax
from jax.experimental import pallas as pl
from jax.experimental.pallas import tpu as pltpu

F32 = jnp.float32
BF16 = jnp.bfloat16

HEAD_DIM = 64
GRID_W = 64
WIN_H = 8
WIN_W = 16
N_EXPERTS = 32
TOP_K = 4
SWIGLU_ALPHA = 1.702
SWIGLU_LIMIT = 7.0
RMS_EPS = 1e-5
GN_EPS = 64e-5
D_LORA = 64
D_GATE_LORA = 128
CHUNK = 64
LANES = 128
MASK_NEG = -1e30
VMEM_LIMIT = 56 * 1024 * 1024


def _cparams(sem):
    return pltpu.CompilerParams(dimension_semantics=sem, vmem_limit_bytes=VMEM_LIMIT)


def _dot(a, b):
    return jnp.dot(a, b, preferred_element_type=F32)


def _split2(x):
    hi = x.astype(BF16)
    lo = (x - hi.astype(F32)).astype(BF16)
    return hi, lo


def _dot_exact_lhs(a01, x):
    hi, lo = _split2(x)
    return _dot(a01, hi) + _dot(a01, lo)


def _dot_exact_rhs(x, b01):
    hi, lo = _split2(x)
    return _dot(hi, b01) + _dot(lo, b01)


def _inproj_kernel(x_ref, nw_ref, w_ref, qkv_ref, p_ref):
    x = x_ref[...]
    ms = jnp.mean(x * x, axis=-1, keepdims=True)
    hn = (x * lax.rsqrt(ms + RMS_EPS) * nw_ref[...]).astype(BF16)
    n_qkv = qkv_ref.shape[-1]
    qkv_ref[...] = _dot(hn, w_ref[:, :n_qkv]).astype(BF16)
    p_ref[...] = _dot(hn, w_ref[:, n_qkv:]).astype(BF16)


def _in_proj(x2d, norm_w, w_in_bf16, n_qkv, tm=512):
    M, D = x2d.shape
    n_all = w_in_bf16.shape[1]
    n_p = n_all - n_qkv
    return pl.pallas_call(
        _inproj_kernel,
        grid=(M // tm,),
        in_specs=[
            pl.BlockSpec((tm, D), lambda i: (i, 0)),
            pl.BlockSpec((1, D), lambda i: (0, 0)),
            pl.BlockSpec((D, n_all), lambda i: (0, 0)),
        ],
        out_specs=[
            pl.BlockSpec((tm, n_qkv), lambda i: (i, 0)),
            pl.BlockSpec((tm, n_p), lambda i: (i, 0)),
        ],
        out_shape=[jax.ShapeDtypeStruct((M, n_qkv), BF16), jax.ShapeDtypeStruct((M, n_p), BF16)],
        compiler_params=_cparams(("parallel",)),
        name="in_proj",
    )(x2d, norm_w.reshape(1, D), w_in_bf16)


def _na_bias_table(rpb):
    H = rpb.shape[0]
    cols = np.arange(GRID_W)
    col_start = np.clip(cols - WIN_W // 2, 0, GRID_W - WIN_W)
    kc = np.arange(GRID_W)
    valid = (kc[None, :] >= col_start[:, None]) & (kc[None, :] < col_start[:, None] + WIN_W)
    dc = kc[None, :] - cols[:, None] + (WIN_W - 1)
    var = np.arange(WIN_H)
    wi = np.arange(WIN_H)
    dr = wi[None, :] - var[:, None] + (WIN_H - 1)
    row_sel = (dr[:, :, None] == np.arange(2 * WIN_H - 1)).astype(np.float32)
    col_sel = ((dc[:, :, None] == np.arange(2 * WIN_W - 1)) & valid[:, :, None]).astype(np.float32)
    tab = jnp.einsum("vir,hrd,xyd->vhxiy", row_sel, rpb.astype(F32), col_sel, precision=lax.Precision.HIGHEST)
    tab = jnp.where(valid[None, None, :, None, :], tab, MASK_NEG)
    return tab.reshape(WIN_H, H, GRID_W, WIN_H * GRID_W)


def _na_kernel(q_ref, k_ref, v_ref, bias_ref, o_ref, *, rows_per_step, n_rows):
    i = pl.program_id(2)
    lane = lax.broadcasted_iota(jnp.int32, (1, LANES), 1)
    head0 = lane < HEAD_DIM
    scale = HEAD_DIM ** -0.5
    nkeys = WIN_H * GRID_W

    def scores(rr):
        r = i * rows_per_step + rr
        r_start = jnp.clip(r - WIN_H // 2, 0, n_rows - WIN_H)
        var = r - r_start
        q = q_ref[0, rr * GRID_W:(rr + 1) * GRID_W, :]
        koff = pl.multiple_of(r_start * GRID_W, GRID_W)
        kwin = k_ref[0, pl.ds(koff, nkeys), :]
        zero = jnp.zeros_like(q)
        q2 = jnp.concatenate([jnp.where(head0, q, zero), jnp.where(head0, zero, q)], axis=0)
        s = lax.dot_general(q2, kwin, (((1,), (1,)), ((), ())), preferred_element_type=F32)
        return koff, s * scale + bias_ref[var].reshape(2 * GRID_W, nkeys)

    def finish(rr, koff, s):
        vwin = v_ref[0, pl.ds(koff, nkeys), :]
        m = jnp.max(s, axis=-1, keepdims=True)
        e = jnp.exp(s - m)
        denom = jnp.sum(e, axis=-1, keepdims=True)
        o = _dot(e.astype(BF16), vwin) / denom
        o_ref[0, rr * GRID_W:(rr + 1) * GRID_W, :] = jnp.where(head0, o[:GRID_W], o[GRID_W:]).astype(o_ref.dtype)

    ahead = 3
    pending = [scores(rr) for rr in range(min(ahead, rows_per_step))]
    for rr in range(rows_per_step):
        if rr + ahead < rows_per_step:
            pending.append(scores(rr + ahead))
        finish(rr, *pending.pop(0))


def _na2d(qkv, bias_tab, att_w, rows_per_step=16):
    B, T, _ = qkv.shape
    n_rows = T // GRID_W
    assert n_rows >= WIN_H and n_rows % rows_per_step == 0
    n_pairs = att_w // LANES
    tq = rows_per_step * GRID_W
    kernel = functools.partial(_na_kernel, rows_per_step=rows_per_step, n_rows=n_rows)
    return pl.pallas_call(
        kernel,
        grid=(B, n_pairs, n_rows // rows_per_step),
        in_specs=[
            pl.BlockSpec((1, tq, LANES), lambda b, hp, i: (b, i, hp)),
            pl.BlockSpec((1, T, LANES), lambda b, hp, i: (b, 0, n_pairs + hp)),
            pl.BlockSpec((1, T, LANES), lambda b, hp, i: (b, 0, 2 * n_pairs + hp)),
            pl.BlockSpec((WIN_H, 2, GRID_W, WIN_H * GRID_W), lambda b, hp, i: (0, hp, 0, 0)),
        ],
        out_specs=pl.BlockSpec((1, tq, LANES), lambda b, hp, i: (b, i, hp)),
        out_shape=jax.ShapeDtypeStruct((B, T, att_w), BF16),
        compiler_params=_cparams(("parallel", "parallel", "arbitrary")),
        name="na2d",
    )(qkv, qkv, qkv, bias_tab)


def _prep_kernel(p_ref, pprev_ref, pnext_ref, mup_ref, mun_ref, w0_ref, w2_ref, a0_ref, a2_ref, g2_ref,
                 kk_ref, ka_ref, rk_ref, bd_ref, tri_ref,
                 kq_ref, rq_ref, kd_ref, bk_ref, egl_ref, v_ref, bonus_ref, gate_ref, *, tb, c):
    i = pl.program_id(1)
    nb = pl.num_programs(1)
    p = p_ref[0].astype(F32)
    halo = pprev_ref.shape[1]
    prow = jnp.where(i > 0, pprev_ref[0, halo - 1:halo, :].astype(F32), 0.0)
    nrow = jnp.where(i < nb - 1, pnext_ref[0, 0:1, :].astype(F32), 0.0)
    ridx = lax.broadcasted_iota(jnp.int32, (tb, 1), 0)
    p_prev = jnp.where(ridx == 0, prow, pltpu.roll(p, 1, 0))
    p_next = jnp.where(ridx == tb - 1, nrow, pltpu.roll(p, tb - 1, 0))
    pm = p + mup_ref[...] * (p_prev - p) + mun_ref[...] * (p_next - p)
    r = pm[:, :c]
    k = pm[:, c:2 * c]
    v = pm[:, 2 * c:3 * c]
    o = 3 * c
    dw = pm[:, o:o + 2 * D_LORA]
    da = pm[:, o + 2 * D_LORA:o + 4 * D_LORA]
    dg = pm[:, o + 4 * D_LORA:o + 4 * D_LORA + D_GATE_LORA]

    wl = w0_ref[...] + _dot(jnp.tanh(dw).astype(BF16), w2_ref[...])
    neg = -wl
    softplus = jnp.maximum(neg, 0.0) + jnp.log(1.0 + jnp.exp(-jnp.abs(neg)))
    logw = -jnp.exp(-softplus - 0.5)
    a = jax.nn.sigmoid(a0_ref[...] + _dot(da.astype(BF16), a2_ref[...]))
    gate = _dot(jax.nn.sigmoid(dg).astype(BF16), g2_ref[...])

    bd = bd_ref[...]
    kkr = k * kk_ref[...]
    ssq = _dot_exact_rhs(kkr * kkr, bd)
    kk = kkr / jnp.maximum(jnp.sqrt(ssq), 1e-12)

    ksum = jnp.zeros_like(k)
    nchunk = tb // CHUNK
    for z in range(2):
        a_z = a[:, z * c:(z + 1) * c]
        lw = logw[:, z * c:(z + 1) * c]
        kdir = k * (1.0 + (a_z - 1.0) * ka_ref[...])
        ksum = ksum + kdir
        b_z = a_z * kk
        g_incl = _dot_exact_lhs(tri_ref[z], lw)
        last = CHUNK - 1 if z == 0 else 0
        g_all = g_incl.reshape(nchunk, CHUNK, c)[:, last:last + 1, :]
        e_neg = jnp.exp(-g_incl)
        kq_ref[z, 0] = (kk * jnp.exp(g_incl - lw)).astype(BF16)
        rq_ref[z, 0] = (r * jnp.exp(g_incl)).astype(BF16)
        kd_ref[z, 0] = (kdir * e_neg).astype(BF16)
        bk_ref[z, 0] = (b_z * e_neg).astype(BF16)
        egl_ref[z, 0] = jnp.exp(g_all)
    coef = _dot_exact_rhs(r * ksum * rk_ref[...], bd)
    v_ref[0] = v.astype(BF16)
    bonus_ref[0] = (coef * v).astype(BF16)
    gate_ref[0] = gate.astype(BF16)


def _rwkv_prep(p, mu_prev, mu_next, w0, w2, a0, a2, g2, k_k, k_a, r_k, tb=256):
    B, T, pw = p.shape
    c = k_k.shape[0]
    assert T % tb == 0 and tb % CHUNK == 0
    halo = 16
    nhb = T // halo
    nchunk = tb // CHUNK
    w2bd = jnp.zeros((2 * D_LORA, 2 * c), F32)
    a2bd = jnp.zeros((2 * D_LORA, 2 * c), F32)
    for z in range(2):
        w2bd = w2bd.at[z * D_LORA:(z + 1) * D_LORA, z * c:(z + 1) * c].set(w2[z])
        a2bd = a2bd.at[z * D_LORA:(z + 1) * D_LORA, z * c:(z + 1) * c].set(a2[z])
    ch = np.arange(c) // HEAD_DIM
    bd = jnp.asarray((ch[:, None] == ch[None, :]).astype(np.float32), BF16)
    t = np.arange(tb)
    same = (t[:, None] // CHUNK) == (t[None, :] // CHUNK)
    tri = np.stack([same & (t[None, :] <= t[:, None]), same & (t[None, :] >= t[:, None])]).astype(np.float32)
    tri = jnp.asarray(tri, BF16)
    row = lambda u: u.reshape(1, -1).astype(F32)
    const = lambda shape: pl.BlockSpec(shape, lambda b, i: (0,) * len(shape))
    dir_out = pl.BlockSpec((2, 1, tb, c), lambda b, i: (0, b, i, 0))
    tok_out = pl.BlockSpec((1, tb, c), lambda b, i: (b, i, 0))
    kernel = functools.partial(_prep_kernel, tb=tb, c=c)
    return pl.pallas_call(
        kernel,
        grid=(B, T // tb),
        in_specs=[
            pl.BlockSpec((1, tb, pw), lambda b, i: (b, i, 0)),
            pl.BlockSpec((1, halo, pw), lambda b, i: (b, jnp.maximum(i * (tb // halo) - 1, 0), 0)),
            pl.BlockSpec((1, halo, pw), lambda b, i: (b, jnp.minimum((i + 1) * (tb // halo), nhb - 1), 0)),
            const((1, pw)), const((1, pw)),
            const((1, 2 * c)), const((2 * D_LORA, 2 * c)),
            const((1, 2 * c)), const((2 * D_LORA, 2 * c)),
            const((D_GATE_LORA, c)),
            const((1, c)), const((1, c)), const((1, c)),
            const((c, c)), const((2, tb, tb)),
        ],
        out_specs=[dir_out, dir_out, dir_out, dir_out,
                   pl.BlockSpec((2, 1, nchunk, 1, c), lambda b, i: (0, b, i, 0, 0)),
                   tok_out, tok_out, tok_out],
        out_shape=[jax.ShapeDtypeStruct((2, B, T, c), BF16)] * 4
        + [jax.ShapeDtypeStruct((2, B, T // CHUNK, 1, c), F32)]
        + [jax.ShapeDtypeStruct((B, T, c), BF16)] * 3,
        compiler_params=_cparams(("parallel", "parallel")),
        name="rwkv_prep",
    )(p, p, p, row(mu_prev), row(mu_next), row(w0), w2bd.astype(BF16), row(a0), a2bd.astype(BF16),
      g2.astype(BF16), row(k_k), row(k_a), row(r_k), bd, tri)


def _bmm(a, b):
    return jnp.einsum("cij,cjk->cik", a.astype(BF16), b.astype(BF16), preferred_element_type=F32)


def _bmm_nt(a, b):
    return jnp.einsum("cik,cjk->cij", a.astype(BF16), b.astype(BF16), preferred_element_type=F32)


def _scan_kernel(*refs, cg, n_pairs):
    in_refs = (refs[:6], refs[6:12])
    y_refs = refs[12:14]
    h_s, mc_s, cc_s, hs_s = refs[14:]
    gi = pl.program_id(2)

    @pl.when(gi == 0)
    def _():
        h_s[...] = jnp.zeros_like(h_s)

    two = 2 * CHUNK
    lane = lax.broadcasted_iota(jnp.int32, (1, 1, LANES), 2)
    head0 = lane < HEAD_DIM
    row = lax.broadcasted_iota(jnp.int32, (1, two, two), 1)
    col = lax.broadcasted_iota(jnp.int32, (1, two, two), 2)
    same = (row // CHUNK) == (col // CHUNK)
    ahead = (row % CHUNK) - (col % CHUNK)
    eye = (row == col).astype(F32)
    chains = [(z, pp) for z in range(2) for pp in range(n_pairs)]

    def stacked(ref, pp):
        x = ref[0, :, pp * LANES:(pp + 1) * LANES].reshape(cg, CHUNK, LANES)
        zero = jnp.zeros_like(x)
        return jnp.concatenate([jnp.where(head0, x, zero), jnp.where(head0, zero, x)], axis=1)

    ops = []
    for z, pp in chains:
        kq_ref, rq_ref, kd_ref, bk_ref, v_ref, egl_ref = in_refs[z]
        ops.append(dict(
            kq=stacked(kq_ref.at[0], pp), rq=stacked(rq_ref.at[0], pp), kd=stacked(kd_ref.at[0], pp),
            bk=stacked(bk_ref.at[0], pp), v=stacked(v_ref, pp),
            egl=egl_ref[0, 0, :, :, pp * LANES:(pp + 1) * LANES],
            strict=same & ((ahead > 0) if z == 0 else (ahead < 0)),
            incl=same & ((ahead >= 0) if z == 0 else (ahead <= 0))))

    for o in ops:
        gram = _bmm_nt(jnp.concatenate([o["kq"], o["rq"]], axis=1), jnp.concatenate([o["kd"], o["bk"]], axis=1))
        o["a_kk"] = jnp.where(o["strict"], gram[:, :two, :two], 0.0)
        o["pw"] = -jnp.where(o["strict"], gram[:, :two, two:], 0.0)
        o["a_rk"] = jnp.where(o["incl"], gram[:, two:, :two], 0.0)
        o["a_rb"] = jnp.where(o["incl"], gram[:, two:, two:], 0.0)
        o["tinv"] = eye + o["pw"]

    for o in ops:
        o["pw"] = _bmm(o["pw"], o["pw"])
    for _ in range(4):
        for o in ops:
            both = _bmm(o["pw"], jnp.concatenate([o["pw"].astype(BF16), o["tinv"].astype(BF16)], axis=2))
            o["pw"] = both[:, :, :two]
            o["tinv"] = o["tinv"] + both[:, :, two:]
    for o in ops:
        o["tinv"] = o["tinv"] + _bmm(o["pw"], o["tinv"])
    for o in ops:
        o["akv"] = _bmm(o["a_kk"], o["v"])
    for o in ops:
        o["wu"] = _bmm(o["tinv"], jnp.concatenate([o["kq"], o["akv"].astype(BF16)], axis=2))
    for n, o in enumerate(ops):
        bkg_t = jnp.swapaxes(o["bk"].astype(F32) * o["egl"], 1, 2)
        kdg_t = jnp.swapaxes(o["kd"].astype(F32) * o["egl"], 1, 2)
        bwu = _bmm(bkg_t, o["wu"])
        mc_s[n] = (eye * o["egl"] - bwu[:, :, :LANES]).astype(BF16)
        cc_s[n] = _bmm(kdg_t, o["v"]) - bwu[:, :, LANES:]
    for o in ops:
        rwu = _bmm(o["a_rb"], o["wu"])
        o["q"] = o["rq"].astype(F32) - rwu[:, :, :LANES]
        o["y0"] = _bmm(o["a_rk"], o["v"]) - rwu[:, :, LANES:]

    def body(ci, carry):
        for n, (z, _) in enumerate(chains):
            cidx = ci if z == 0 else cg - 1 - ci
            hb = h_s[n].astype(BF16)
            hs_s[n, cidx] = hb
            h_s[n] = _dot(mc_s[n, cidx], hb) + cc_s[n, cidx]
        return carry

    lax.fori_loop(0, cg, body, 0)
    for n, (z, pp) in enumerate(chains):
        y = _bmm(ops[n]["q"], hs_s[n]) + ops[n]["y0"]
        y = (y[:, :CHUNK, :] + y[:, CHUNK:, :]).reshape(cg * CHUNK, LANES)
        y_refs[z][0, :, pp * LANES:(pp + 1) * LANES] = y.astype(y_refs[z].dtype)


def _rwkv_scan(kq, rq, kd, bk, v, egl, cg=8, n_pairs=2):
    _, B, T, c = kq.shape
    lw = n_pairs * LANES
    tl = cg * CHUNK
    ng = T // tl
    assert T % tl == 0 and c % lw == 0
    n_chain = 2 * n_pairs

    in_specs, args = [], []
    for z in range(2):
        tmap = (lambda g: g) if z == 0 else (lambda g: ng - 1 - g)
        dir_in = pl.BlockSpec((1, 1, tl, lw), lambda b, hp, g, z=z, tmap=tmap: (z, b, tmap(g), hp))
        in_specs += [dir_in] * 4
        in_specs.append(pl.BlockSpec((1, tl, lw), lambda b, hp, g, tmap=tmap: (b, tmap(g), hp)))
        in_specs.append(pl.BlockSpec((1, 1, cg, 1, lw), lambda b, hp, g, z=z, tmap=tmap: (z, b, tmap(g), 0, hp)))
        args += [kq, rq, kd, bk, v, egl]
    out_specs = [pl.BlockSpec((1, tl, lw), lambda b, hp, g: (b, g, hp)),
                 pl.BlockSpec((1, tl, lw), lambda b, hp, g: (b, ng - 1 - g, hp))]
    kernel = functools.partial(_scan_kernel, cg=cg, n_pairs=n_pairs)
    return pl.pallas_call(
        kernel,
        grid=(B, c // lw, ng),
        in_specs=in_specs,
        out_specs=out_specs,
        out_shape=[jax.ShapeDtypeStruct((B, T, c), BF16)] * 2,
        scratch_shapes=[pltpu.VMEM((n_chain, LANES, LANES), F32),
                        pltpu.VMEM((n_chain, cg, LANES, LANES), BF16),
                        pltpu.VMEM((n_chain, cg, LANES, LANES), F32),
                        pltpu.VMEM((n_chain, cg, LANES, LANES), BF16)],
        compiler_params=_cparams(("parallel", "parallel", "arbitrary")),
        name="rwkv_scan",
    )(*args)


def _mix_kernel(x_ref, att_ref, yf_ref, yb_ref, bonus_ref, gate_ref, anw_ref, lnw_ref, lnb_ref, bd_ref, wo_ref,
                n2w_ref, rw_ref, rb_ref, tril_ref, h_ref, hn_ref, idx_ref, gates_ref, rank_ref, cnt_ref, cnt_s):
    @pl.when(pl.program_id(0) == 0)
    def _():
        cnt_s[...] = jnp.zeros_like(cnt_s)

    bd = bd_ref[...]
    inv_n = 1.0 / HEAD_DIM
    att_w = att_ref.shape[-1]
    rows = tril_ref.shape[0]
    lane = lax.broadcasted_iota(jnp.int32, (rows, LANES), 1)
    cnt = cnt_s[...]
    n_sub = x_ref.shape[0] // rows
    all_logits = []
    for j in range(n_sub):
        sl = slice(j * rows, (j + 1) * rows)
        att = att_ref[sl, :].astype(F32)
        ms = jnp.mean(att * att, axis=-1, keepdims=True)
        att_n = att * lax.rsqrt(ms + RMS_EPS) * anw_ref[...]

        y = yf_ref[sl, :].astype(F32) + yb_ref[sl, :].astype(F32)
        mean = _dot_exact_rhs(y, bd) * inv_n
        yc = y - mean
        var = _dot_exact_rhs(yc * yc, bd) * inv_n
        yn = yc * lax.rsqrt(var + GN_EPS) * lnw_ref[...] + lnb_ref[...]
        rk = (yn + bonus_ref[sl, :].astype(F32)) * gate_ref[sl, :].astype(F32)

        mix = _dot(att_n.astype(BF16), wo_ref[:att_w, :]) + _dot(rk.astype(BF16), wo_ref[att_w:, :])
        h = x_ref[sl, :] + mix
        h_ref[sl, :] = h
        ms2 = jnp.mean(h * h, axis=-1, keepdims=True)
        hn = h * lax.rsqrt(ms2 + RMS_EPS) * n2w_ref[...]
        hn_ref[sl, :] = hn

        hn_hi, hn_lo = _split2(hn)
        all_logits.append(_dot(hn_hi, rw_ref[0]) + _dot(hn_lo, rw_ref[0]) + _dot(hn_hi, rw_ref[1]) + rb_ref[...])

    for j, logits in enumerate(all_logits):
        sl = slice(j * rows, (j + 1) * rows)
        cur = logits
        vals, idxs = [], []
        for _ in range(TOP_K):
            m = jnp.max(cur, axis=-1, keepdims=True)
            sel = jnp.min(jnp.where(cur == m, lane, LANES), axis=-1, keepdims=True)
            vals.append(m)
            idxs.append(sel)
            cur = jnp.where(lane == sel, -jnp.inf, cur)
        es = [jnp.exp(vk - vals[0]) for vk in vals]
        tot = es[0] + es[1] + es[2] + es[3]
        onehot = jnp.zeros(logits.shape, F32)
        for kk in range(TOP_K):
            onehot = onehot + (lane == idxs[kk]).astype(F32)
        before = cnt + _dot(tril_ref[...], onehot.astype(BF16))
        cnt = cnt + jnp.sum(onehot, axis=0, keepdims=True)

        idx_out = jnp.zeros(logits.shape, jnp.int32)
        g_out = jnp.zeros(logits.shape, F32)
        rank_out = jnp.zeros(logits.shape, F32)
        for kk in range(TOP_K):
            idx_out = jnp.where(lane == kk, idxs[kk], idx_out)
            g_out = jnp.where(lane == kk, es[kk] / tot, g_out)
            rank_k = jnp.sum(jnp.where(lane == idxs[kk], before, 0.0), axis=-1, keepdims=True)
            rank_out = jnp.where(lane == kk, rank_k, rank_out)
        idx_ref[sl, :] = idx_out
        gates_ref[sl, :] = g_out
        rank_ref[sl, :] = rank_out.astype(jnp.int32)
    cnt_s[...] = cnt
    cnt_ref[...] = cnt


def _mix(x2d, att, y_fwd, y_bwd, bonus, gate, attn_norm_w, ln_w, ln_b, w_o, norm2_w, router_w, router_b,
         tm=512, n_split=2):
    M, D = x2d.shape
    c = y_fwd.shape[-1]
    att_w = att.shape[-1]
    ch = np.arange(c) // HEAD_DIM
    bd = jnp.asarray((ch[:, None] == ch[None, :]).astype(np.float32), BF16)
    t = np.arange(tm // n_split)
    tril = jnp.asarray((t[None, :] < t[:, None]).astype(np.float32), BF16)
    rw = jnp.zeros((D, LANES), F32).at[:, :N_EXPERTS].set(router_w)
    rw = jnp.stack(_split2(rw))
    rb = jnp.full((1, LANES), -jnp.inf, F32).at[0, :N_EXPERTS].set(router_b)
    row = lambda u: u.reshape(1, -1).astype(F32)
    const = lambda shape: pl.BlockSpec(shape, lambda i: (0,) * len(shape))
    tok = lambda w: pl.BlockSpec((tm, w), lambda i: (i, 0))
    return pl.pallas_call(
        _mix_kernel,
        grid=(M // tm,),
        in_specs=[
            tok(D), tok(att_w), tok(c), tok(c), tok(c), tok(c),
            const((1, att_w)), const((1, c)), const((1, c)), const((c, c)), const((att_w + c, D)),
            const((1, D)), const((2, D, LANES)), const((1, LANES)), const((tm // n_split, tm // n_split)),
        ],
        out_specs=[
            tok(D), tok(D),
            tok(LANES), tok(LANES), tok(LANES),
            const((1, LANES)),
        ],
        out_shape=[jax.ShapeDtypeStruct((M, D), F32), jax.ShapeDtypeStruct((M, D), F32),
                   jax.ShapeDtypeStruct((M, LANES), jnp.int32), jax.ShapeDtypeStruct((M, LANES), F32),
                   jax.ShapeDtypeStruct((M, LANES), jnp.int32), jax.ShapeDtypeStruct((1, LANES), F32)],
        scratch_shapes=[pltpu.VMEM((1, LANES), F32)],
        compiler_params=_cparams(("arbitrary",)),
        name="mix_router",
    )(x2d, att, y_fwd, y_bwd, bonus, gate, row(attn_norm_w), row(ln_w), row(ln_b), bd, w_o.astype(BF16),
      row(norm2_w), rw, rb, tril)


def _route(top_idx, rank, counts, blk):
    M = top_idx.shape[0]
    counts = counts.astype(jnp.int32)
    padded = (counts + blk - 1) // blk * blk
    pend = jnp.cumsum(padded)
    pstart = pend - padded
    experts = jnp.arange(N_EXPERTS, dtype=jnp.int32)
    first = jnp.sum(jnp.where(top_idx[:, :, None] == experts, pstart, 0), axis=-1)
    dest = (first + rank).astype(jnp.int32)
    n_blk = (M * TOP_K + N_EXPERTS * blk) // blk
    starts = jnp.arange(n_blk, dtype=jnp.int32) * blk
    blk_expert = jnp.sum((pend[None, :] <= starts[:, None]).astype(jnp.int32), axis=1)
    blk_expert = jnp.minimum(blk_expert, N_EXPERTS - 1).astype(jnp.int32)
    n_used = (pend[-1] // blk).reshape(1).astype(jnp.int32)
    return dest, counts, padded.astype(jnp.int32), pstart.astype(jnp.int32), blk_expert, n_used


def _dispatch_kernel(cnt_ref, pad_ref, first_ref, dest_ref, hn_ref, xb_ref, dest_smem, ring, zero_s, sem_idx, sem_in,
                     sem_rows, sem_pad, *, tm):
    i = pl.program_id(0)
    n = pl.num_programs(0)
    slot = i % 3

    def tile_copy(j, s):
        return pltpu.make_async_copy(hn_ref.at[pl.ds(j * tm, tm)], ring.at[s], sem_in.at[s])

    def rows_wait(step):
        for kk in range(TOP_K):
            pltpu.make_async_copy(ring.at[0], xb_ref.at[pl.ds(0, tm)], sem_rows.at[step % 2]).wait()

    @pl.when(i == 0)
    def _():
        tile_copy(0, 0).start()

    cp = pltpu.make_async_copy(dest_ref.at[0], dest_smem, sem_idx)
    cp.start()

    @pl.when(i + 1 < n)
    def _():
        tile_copy(i + 1, (i + 1) % 3).start()

    tile_copy(i, slot).wait()
    cp.wait()

    def issue(t, carry):
        for kk in range(TOP_K):
            row = dest_smem[0, t * TOP_K + kk]
            pltpu.make_async_copy(ring.at[slot, pl.ds(t, 1)], xb_ref.at[pl.ds(row, 1)],
                                  sem_rows.at[i % 2]).start(priority=kk % 2)
        return carry

    lax.fori_loop(0, tm, issue, 0)

    @pl.when(i == 0)
    def _():
        zero_s[...] = jnp.zeros_like(zero_s)

        def zero_rows(lo, hi):
            def start(r, c2):
                pltpu.make_async_copy(zero_s.at[pl.ds(0, 1)], xb_ref.at[pl.ds(r, 1)], sem_pad).start()
                return c2

            def wait(r, c2):
                pltpu.make_async_copy(zero_s.at[pl.ds(0, 1)], xb_ref.at[pl.ds(r, 1)], sem_pad).wait()
                return c2

            lax.fori_loop(lo, hi, start, 0)
            lax.fori_loop(lo, hi, wait, 0)

        def per_expert(e, carry):
            zero_rows(first_ref[e] + cnt_ref[e], first_ref[e] + pad_ref[e])
            return carry

        lax.fori_loop(0, N_EXPERTS, per_expert, 0)
        zero_rows(first_ref[N_EXPERTS - 1] + pad_ref[N_EXPERTS - 1], xb_ref.shape[0])

    @pl.when(i > 0)
    def _():
        rows_wait(i - 1)

    @pl.when(i == n - 1)
    def _():
        rows_wait(i)


def _dispatch(hn, dest, counts, padded, pstart, n_rows, tm=256):
    M, D = hn.shape
    kernel = functools.partial(_dispatch_kernel, tm=tm)
    grid_spec = pltpu.PrefetchScalarGridSpec(
        num_scalar_prefetch=3,
        grid=(M // tm,),
        in_specs=[pl.BlockSpec((1, 1, TOP_K * tm), lambda i, *_: (i, 0, 0)),
                  pl.BlockSpec(memory_space=pl.ANY)],
        out_specs=pl.BlockSpec(memory_space=pl.ANY),
        scratch_shapes=[pltpu.SMEM((1, TOP_K * tm), jnp.int32), pltpu.VMEM((3, tm, D), F32), pltpu.VMEM((8, D), F32),
                        pltpu.SemaphoreType.DMA(()), pltpu.SemaphoreType.DMA((3,)), pltpu.SemaphoreType.DMA((2,)),
                        pltpu.SemaphoreType.DMA(())],
    )
    return pl.pallas_call(
        kernel,
        grid_spec=grid_spec,
        out_shape=jax.ShapeDtypeStruct((n_rows, D), F32),
        compiler_params=_cparams(("arbitrary",)),
        name="dispatch",
    )(counts, padded, pstart, dest.reshape(M // tm, 1, TOP_K * tm), hn)


def _expert_kernel(be_ref, nu_ref, x_ref, wu_ref, bu_ref, wd_ref, bdn_ref, y_ref, wu_s, wd_s, *, f, n_split):
    i = pl.program_id(0)
    e = be_ref[i]
    changed = jnp.logical_or(i == 0, e != be_ref[jnp.maximum(i - 1, 0)])

    @pl.when(changed)
    def _():
        wu_s[...] = wu_ref[0].astype(BF16)
        wd_s[...] = wd_ref[0].astype(BF16)

    @pl.when(i < nu_ref[0])
    def _():
        rows = x_ref.shape[0] // n_split
        hcats = [_dot(x_ref[j * rows:(j + 1) * rows, :].astype(BF16), wu_s[...]) + bu_ref[0] for j in range(n_split)]
        for j, hcat in enumerate(hcats):
            gate = jnp.minimum(hcat[:, :f], SWIGLU_LIMIT)
            lin = jnp.clip(hcat[:, f:], -SWIGLU_LIMIT, SWIGLU_LIMIT)
            act = (lin + 1.0) * (gate * jax.nn.sigmoid(SWIGLU_ALPHA * gate))
            y_ref[j * rows:(j + 1) * rows, :] = _dot(act.astype(BF16), wd_s[...]) + bdn_ref[0]

    @pl.when(i >= nu_ref[0])
    def _():
        y_ref[...] = jnp.zeros_like(y_ref)


def _experts(x_buf, blk_expert, n_used, w_up, b_up, w_down, b_down, blk, n_split=2):
    P, D = x_buf.shape
    E, _, f2 = w_up.shape
    f = f2 // 2
    kernel = functools.partial(_expert_kernel, f=f, n_split=n_split)
    grid_spec = pltpu.PrefetchScalarGridSpec(
        num_scalar_prefetch=2,
        grid=(P // blk,),
        in_specs=[
            pl.BlockSpec((blk, D), lambda i, be, nu: (jnp.minimum(i, nu[0] - 1), 0)),
            pl.BlockSpec((1, D, f2), lambda i, be, nu: (be[i], 0, 0)),
            pl.BlockSpec((1, 1, f2), lambda i, be, nu: (be[i], 0, 0)),
            pl.BlockSpec((1, f, D), lambda i, be, nu: (be[i], 0, 0)),
            pl.BlockSpec((1, 1, D), lambda i, be, nu: (be[i], 0, 0)),
        ],
        out_specs=pl.BlockSpec((blk, D), lambda i, be, nu: (i, 0)),
        scratch_shapes=[pltpu.VMEM((D, f2), BF16), pltpu.VMEM((f, D), BF16)],
    )
    return pl.pallas_call(
        kernel,
        grid_spec=grid_spec,
        out_shape=jax.ShapeDtypeStruct((P, D), F32),
        compiler_params=_cparams(("arbitrary",)),
        name="experts",
    )(blk_expert, n_used, x_buf, w_up, b_up.reshape(E, 1, f2), w_down, b_down.reshape(E, 1, D))


def _combine_kernel(dfirst_ref, dnext_ref, h_ref, g_ref, fw_ref, yb_ref, o_ref, dest_smem, yg_s, sem_idx, sem, *, tm):
    i = pl.program_id(0)
    n = pl.num_programs(0)

    def gather_tile(dref, slot):
        cp = pltpu.make_async_copy(dref.at[0], dest_smem, sem_idx)
        cp.start()
        cp.wait()

        def issue(t, carry):
            for kk in range(TOP_K):
                row = dest_smem[0, t * TOP_K + kk]
                pltpu.make_async_copy(yb_ref.at[pl.ds(row, 1)], yg_s.at[slot, kk, pl.ds(t, 1)],
                                      sem.at[slot]).start(priority=kk % 2)
            return carry

        lax.fori_loop(0, tm, issue, 0)

    @pl.when(i == 0)
    def _():
        gather_tile(dfirst_ref, 0)

    @pl.when(i + 1 < n)
    def _():
        gather_tile(dnext_ref, (i + 1) % 2)

    slot = i % 2
    g = g_ref[...]
    acc = h_ref[...]
    for kk in range(TOP_K):
        pltpu.make_async_copy(yb_ref.at[pl.ds(0, tm)], yg_s.at[slot, kk], sem.at[slot]).wait()
    for kk in range(TOP_K):
        acc = acc + g[:, kk:kk + 1] * yg_s[slot, kk]
    ms = jnp.mean(acc * acc, axis=-1, keepdims=True)
    o_ref[...] = acc * lax.rsqrt(ms + RMS_EPS) * fw_ref[...]


def _combine(h, y_buf, dest, gates, final_w, tm=256):
    M, D = h.shape
    kernel = functools.partial(_combine_kernel, tm=tm)
    n = M // tm
    dest_tiles = dest.reshape(n, 1, TOP_K * tm)
    return pl.pallas_call(
        kernel,
        grid=(n,),
        in_specs=[
            pl.BlockSpec((1, 1, TOP_K * tm), lambda i: (0, 0, 0)),
            pl.BlockSpec((1, 1, TOP_K * tm), lambda i: (jnp.minimum(i + 1, n - 1), 0, 0)),
            pl.BlockSpec((tm, D), lambda i: (i, 0)),
            pl.BlockSpec((tm, LANES), lambda i: (i, 0)),
            pl.BlockSpec((1, D), lambda i: (0, 0)),
            pl.BlockSpec(memory_space=pl.ANY),
        ],
        out_specs=pl.BlockSpec((tm, D), lambda i: (i, 0)),
        out_shape=jax.ShapeDtypeStruct((M, D), F32),
        scratch_shapes=[pltpu.SMEM((1, TOP_K * tm), jnp.int32), pltpu.VMEM((2, TOP_K, tm, D), F32),
                        pltpu.SemaphoreType.DMA(()), pltpu.SemaphoreType.DMA((2,))],
        compiler_params=_cparams(("arbitrary",)),
        name="combine",
    )(dest_tiles, dest_tiles, h, gates, final_w.reshape(1, D).astype(F32), y_buf)


def kernel(x, norm1_w, w_in, attn_rpb, attn_norm_w, rwkv_mu_prev, rwkv_mu_next, rwkv_w0, rwkv_w2, rwkv_a0, rwkv_a2, rwkv_g2, rwkv_k_k, rwkv_k_a, rwkv_r_k, rwkv_ln_w, rwkv_ln_b, w_o, norm2_w, router_w, router_b, expert_w_up, expert_b_up, expert_w_down, expert_b_down, final_norm_w):
    B, T, D = x.shape
    M = B * T
    depth = norm1_w.shape[0]
    assert depth == 1, "the combine stage folds in the final norm, so it must follow the only layer"
    c = rwkv_k_k.shape[-1]
    att_w = attn_norm_w.shape[-1]
    blk = 512
    h = x.reshape(M, D)
    for l in range(depth):
        qkv, p = _in_proj(h, norm1_w[l], w_in[l].astype(BF16), 3 * att_w)
        att = _na2d(qkv.reshape(B, T, 3 * att_w), _na_bias_table(attn_rpb[l]), att_w)
        kq, rq, kd, bk, egl, v, bonus, gate = _rwkv_prep(
            p.reshape(B, T, -1), rwkv_mu_prev[l], rwkv_mu_next[l], rwkv_w0[l].reshape(-1), rwkv_w2[l],
            rwkv_a0[l].reshape(-1), rwkv_a2[l], rwkv_g2[l], rwkv_k_k[l], rwkv_k_a[l], rwkv_r_k[l].reshape(-1))
        y_fwd, y_bwd = _rwkv_scan(kq, rq, kd, bk, v, egl)
        h, hn, idx_pad, gates_pad, rank_pad, counts = _mix(
            h, att.reshape(M, att_w), y_fwd.reshape(M, c), y_bwd.reshape(M, c), bonus.reshape(M, c),
            gate.reshape(M, c),
            attn_norm_w[l], rwkv_ln_w[l], rwkv_ln_b[l], w_o[l], norm2_w[l], router_w[l], router_b[l])
        dest, cnt, padded, pstart, blk_expert, n_used = _route(
            idx_pad[:, :TOP_K], rank_pad[:, :TOP_K], counts[0, :N_EXPERTS], blk)
        n_rows = (M * TOP_K + N_EXPERTS * blk) // blk * blk
        x_buf = _dispatch(hn, dest, cnt, padded, pstart, n_rows)
        y_buf = _experts(x_buf, blk_expert, n_used, expert_w_up[l], expert_b_up[l], expert_w_down[l],
                         expert_b_down[l], blk)
        h = _combine(h, y_buf, dest, gates_pad, final_norm_w)
    return h.reshape(B, T, D)
```

```python
import functools

import numpy as np
import jax
import jax.numpy as jnp
from jax import lax
from jax.experimental import pallas as pl
from jax.experimental.pallas import tpu as pltpu

F32 = jnp.float32
BF16 = jnp.bfloat16

HEAD_DIM = 64
GRID_W = 64
WIN_H = 8
WIN_W = 16
N_EXPERTS = 32
TOP_K = 4
SWIGLU_ALPHA = 1.702
SWIGLU_LIMIT = 7.0
RMS_EPS = 1e-5
GN_EPS = 64e-5
D_LORA = 64
D_GATE_LORA = 128
CHUNK = 64
LANES = 128
MASK_NEG = -1e30
VMEM_LIMIT = 56 * 1024 * 1024


def _cparams(sem):
    return pltpu.CompilerParams(dimension_semantics=sem, vmem_limit_bytes=VMEM_LIMIT)


def _dot(a, b):
    return jnp.dot(a, b, preferred_element_type=F32)


def _split2(x):
    hi = x.astype(BF16)
    lo = (x - hi.astype(F32)).astype(BF16)
    return hi, lo


def _dot_exact_lhs(a01, x):
    hi, lo = _split2(x)
    return _dot(a01, hi) + _dot(a01, lo)


def _dot_exact_rhs(x, b01):
    hi, lo = _split2(x)
    return _dot(hi, b01) + _dot(lo, b01)


def _inproj_kernel(x_ref, nw_ref, w_ref, qkv_ref, p_ref):
    x = x_ref[...]
    ms = jnp.mean(x * x, axis=-1, keepdims=True)
    hn = (x * lax.rsqrt(ms + RMS_EPS) * nw_ref[...]).astype(BF16)
    n_qkv = qkv_ref.shape[-1]
    qkv_ref[...] = _dot(hn, w_ref[:, :n_qkv]).astype(BF16)
    p_ref[...] = _dot(hn, w_ref[:, n_qkv:]).astype(BF16)


def _in_proj(x2d, norm_w, w_in_bf16, n_qkv, tm=512):
    M, D = x2d.shape
    n_all = w_in_bf16.shape[1]
    n_p = n_all - n_qkv
    return pl.pallas_call(
        _inproj_kernel,
        grid=(M // tm,),
        in_specs=[
            pl.BlockSpec((tm, D), lambda i: (i, 0)),
            pl.BlockSpec((1, D), lambda i: (0, 0)),
            pl.BlockSpec((D, n_all), lambda i: (0, 0)),
        ],
        out_specs=[
            pl.BlockSpec((tm, n_qkv), lambda i: (i, 0)),
            pl.BlockSpec((tm, n_p), lambda i: (i, 0)),
        ],
        out_shape=[jax.ShapeDtypeStruct((M, n_qkv), BF16), jax.ShapeDtypeStruct((M, n_p), BF16)],
        compiler_params=_cparams(("parallel",)),
        name="in_proj",
    )(x2d, norm_w.reshape(1, D), w_in_bf16)


def _na_bias_table(rpb):
    H = rpb.shape[0]
    cols = np.arange(GRID_W)
    col_start = np.clip(cols - WIN_W // 2, 0, GRID_W - WIN_W)
    kc = np.arange(GRID_W)
    valid = (kc[None, :] >= col_start[:, None]) & (kc[None, :] < col_start[:, None] + WIN_W)
    dc = kc[None, :] - cols[:, None] + (WIN_W - 1)
    var = np.arange(WIN_H)
    wi = np.arange(WIN_H)
    dr = wi[None, :] - var[:, None] + (WIN_H - 1)
    row_sel = (dr[:, :, None] == np.arange(2 * WIN_H - 1)).astype(np.float32)
    col_sel = ((dc[:, :, None] == np.arange(2 * WIN_W - 1)) & valid[:, :, None]).astype(np.float32)
    tab = jnp.einsum("vir,hrd,xyd->vhxiy", row_sel, rpb.astype(F32), col_sel, precision=lax.Precision.HIGHEST)
    tab = jnp.where(valid[None, None, :, None, :], tab, MASK_NEG)
    return tab.reshape(WIN_H, H, GRID_W, WIN_H * GRID_W)


def _na_kernel(q_ref, k_ref, v_ref, bias_ref, o_ref, *, rows_per_step, n_rows):
    i = pl.program_id(2)
    lane = lax.broadcasted_iota(jnp.int32, (1, LANES), 1)
    head0 = lane < HEAD_DIM
    scale = HEAD_DIM ** -0.5
    nkeys = WIN_H * GRID_W

    def scores(rr):
        r = i * rows_per_step + rr
        r_start = jnp.clip(r - WIN_H // 2, 0, n_rows - WIN_H)
        var = r - r_start
        q = q_ref[0, rr * GRID_W:(rr + 1) * GRID_W, :]
        koff = pl.multiple_of(r_start * GRID_W, GRID_W)
        kwin = k_ref[0, pl.ds(koff, nkeys), :]
        zero = jnp.zeros_like(q)
        q2 = jnp.concatenate([jnp.where(head0, q, zero), jnp.where(head0, zero, q)], axis=0)
        s = lax.dot_general(q2, kwin, (((1,), (1,)), ((), ())), preferred_element_type=F32)
        return koff, s * scale + bias_ref[var].reshape(2 * GRID_W, nkeys)

    def finish(rr, koff, s):
        vwin = v_ref[0, pl.ds(koff, nkeys), :]
        m = jnp.max(s, axis=-1, keepdims=True)
        e = jnp.exp(s - m)
        denom = jnp.sum(e, axis=-1, keepdims=True)
        o = _dot(e.astype(BF16), vwin) / denom
        o_ref[0, rr * GRID_W:(rr + 1) * GRID_W, :] = jnp.where(head0, o[:GRID_W], o[GRID_W:]).astype(o_ref.dtype)

    ahead = 3
    pending = [scores(rr) for rr in range(min(ahead, rows_per_step))]
    for rr in range(rows_per_step):
        if rr + ahead < rows_per_step:
            pending.append(scores(rr + ahead))
        finish(rr, *pending.pop(0))


def _na2d(qkv, bias_tab, att_w, rows_per_step=16):
    B, T, _ = qkv.shape
    n_rows = T // GRID_W
    assert n_rows >= WIN_H and n_rows % rows_per_step == 0
    n_pairs = att_w // LANES
    tq = rows_per_step * GRID_W
    kernel = functools.partial(_na_kernel, rows_per_step=rows_per_step, n_rows=n_rows)
    return pl.pallas_call(
        kernel,
        grid=(B, n_pairs, n_rows // rows_per_step),
        in_specs=[
            pl.BlockSpec((1, tq, LANES), lambda b, hp, i: (b, i, hp)),
            pl.BlockSpec((1, T, LANES), lambda b, hp, i: (b, 0, n_pairs + hp)),
            pl.BlockSpec((1, T, LANES), lambda b, hp, i: (b, 0, 2 * n_pairs + hp)),
            pl.BlockSpec((WIN_H, 2, GRID_W, WIN_H * GRID_W), lambda b, hp, i: (0, hp, 0, 0)),
        ],
        out_specs=pl.BlockSpec((1, tq, LANES), lambda b, hp, i: (b, i, hp)),
        out_shape=jax.ShapeDtypeStruct((B, T, att_w), BF16),
        compiler_params=_cparams(("parallel", "parallel", "arbitrary")),
        name="na2d",
    )(qkv, qkv, qkv, bias_tab)


def _prep_kernel(p_ref, pprev_ref, pnext_ref, mup_ref, mun_ref, w0_ref, w2_ref, a0_ref, a2_ref, g2_ref,
                 kk_ref, ka_ref, rk_ref, bd_ref, tri_ref,
                 kq_ref, rq_ref, kd_ref, bk_ref, egl_ref, v_ref, bonus_ref, gate_ref, *, tb, c):
    i = pl.program_id(1)
    nb = pl.num_programs(1)
    p = p_ref[0].astype(F32)
    halo = pprev_ref.shape[1]
    prow = jnp.where(i > 0, pprev_ref[0, halo - 1:halo, :].astype(F32), 0.0)
    nrow = jnp.where(i < nb - 1, pnext_ref[0, 0:1, :].astype(F32), 0.0)
    ridx = lax.broadcasted_iota(jnp.int32, (tb, 1), 0)
    p_prev = jnp.where(ridx == 0, prow, pltpu.roll(p, 1, 0))
    p_next = jnp.where(ridx == tb - 1, nrow, pltpu.roll(p, tb - 1, 0))
    pm = p + mup_ref[...] * (p_prev - p) + mun_ref[...] * (p_next - p)
    r = pm[:, :c]
    k = pm[:, c:2 * c]
    v = pm[:, 2 * c:3 * c]
    o = 3 * c
    dw = pm[:, o:o + 2 * D_LORA]
    da = pm[:, o + 2 * D_LORA:o + 4 * D_LORA]
    dg = pm[:, o + 4 * D_LORA:o + 4 * D_LORA + D_GATE_LORA]

    wl = w0_ref[...] + _dot(jnp.tanh(dw).astype(BF16), w2_ref[...])
    neg = -wl
    softplus = jnp.maximum(neg, 0.0) + jnp.log(1.0 + jnp.exp(-jnp.abs(neg)))
    logw = -jnp.exp(-softplus - 0.5)
    a = jax.nn.sigmoid(a0_ref[...] + _dot(da.astype(BF16), a2_ref[...]))
    gate = _dot(jax.nn.sigmoid(dg).astype(BF16), g2_ref[...])

    bd = bd_ref[...]
    kkr = k * kk_ref[...]
    ssq = _dot_exact_rhs(kkr * kkr, bd)
    kk = kkr / jnp.maximum(jnp.sqrt(ssq), 1e-12)

    ksum = jnp.zeros_like(k)
    nchunk = tb // CHUNK
    for z in range(2):
        a_z = a[:, z * c:(z + 1) * c]
        lw = logw[:, z * c:(z + 1) * c]
        kdir = k * (1.0 + (a_z - 1.0) * ka_ref[...])
        ksum = ksum + kdir
        b_z = a_z * kk
        g_incl = _dot_exact_lhs(tri_ref[z], lw)
        last = CHUNK - 1 if z == 0 else 0
        g_all = g_incl.reshape(nchunk, CHUNK, c)[:, last:last + 1, :]
        e_neg = jnp.exp(-g_incl)
        kq_ref[z, 0] = (kk * jnp.exp(g_incl - lw)).astype(BF16)
        rq_ref[z, 0] = (r * jnp.exp(g_incl)).astype(BF16)
        kd_ref[z, 0] = (kdir * e_neg).astype(BF16)
        bk_ref[z, 0] = (b_z * e_neg).astype(BF16)
        egl_ref[z, 0] = jnp.exp(g_all)
    coef = _dot_exact_rhs(r * ksum * rk_ref[...], bd)
    v_ref[0] = v.astype(BF16)
    bonus_ref[0] = (coef * v).astype(BF16)
    gate_ref[0] = gate.astype(BF16)


def _rwkv_prep(p, mu_prev, mu_next, w0, w2, a0, a2, g2, k_k, k_a, r_k, tb=256):
    B, T, pw = p.shape
    c = k_k.shape[0]
    assert T % tb == 0 and tb % CHUNK == 0
    halo = 16
    nhb = T // halo
    nchunk = tb // CHUNK
    w2bd = jnp.zeros((2 * D_LORA, 2 * c), F32)
    a2bd = jnp.zeros((2 * D_LORA, 2 * c), F32)
    for z in range(2):
        w2bd = w2bd.at[z * D_LORA:(z + 1) * D_LORA, z * c:(z + 1) * c].set(w2[z])
        a2bd = a2bd.at[z * D_LORA:(z + 1) * D_LORA, z * c:(z + 1) * c].set(a2[z])
    ch = np.arange(c) // HEAD_DIM
    bd = jnp.asarray((ch[:, None] == ch[None, :]).astype(np.float32), BF16)
    t = np.arange(tb)
    same = (t[:, None] // CHUNK) == (t[None, :] // CHUNK)
    tri = np.stack([same & (t[None, :] <= t[:, None]), same & (t[None, :] >= t[:, None])]).astype(np.float32)
    tri = jnp.asarray(tri, BF16)
    row = lambda u: u.reshape(1, -1).astype(F32)
    const = lambda shape: pl.BlockSpec(shape, lambda b, i: (0,) * len(shape))
    dir_out = pl.BlockSpec((2, 1, tb, c), lambda b, i: (0, b, i, 0))
    tok_out = pl.BlockSpec((1, tb, c), lambda b, i: (b, i, 0))
    kernel = functools.partial(_prep_kernel, tb=tb, c=c)
    return pl.pallas_call(
        kernel,
        grid=(B, T // tb),
        in_specs=[
            pl.BlockSpec((1, tb, pw), lambda b, i: (b, i, 0)),
            pl.BlockSpec((1, halo, pw), lambda b, i: (b, jnp.maximum(i * (tb // halo) - 1, 0), 0)),
            pl.BlockSpec((1, halo, pw), lambda b, i: (b, jnp.minimum((i + 1) * (tb // halo), nhb - 1), 0)),
            const((1, pw)), const((1, pw)),
            const((1, 2 * c)), const((2 * D_LORA, 2 * c)),
            const((1, 2 * c)), const((2 * D_LORA, 2 * c)),
            const((D_GATE_LORA, c)),
            const((1, c)), const((1, c)), const((1, c)),
            const((c, c)), const((2, tb, tb)),
        ],
        out_specs=[dir_out, dir_out, dir_out, dir_out,
                   pl.BlockSpec((2, 1, nchunk, 1, c), lambda b, i: (0, b, i, 0, 0)),
                   tok_out, tok_out, tok_out],
        out_shape=[jax.ShapeDtypeStruct((2, B, T, c), BF16)] * 4
        + [jax.ShapeDtypeStruct((2, B, T // CHUNK, 1, c), F32)]
        + [jax.ShapeDtypeStruct((B, T, c), BF16)] * 3,
        compiler_params=_cparams(("parallel", "parallel")),
        name="rwkv_prep",
    )(p, p, p, row(mu_prev), row(mu_next), row(w0), w2bd.astype(BF16), row(a0), a2bd.astype(BF16),
      g2.astype(BF16), row(k_k), row(k_a), row(r_k), bd, tri)


def _bmm(a, b):
    return jnp.einsum("cij,cjk->cik", a.astype(BF16), b.astype(BF16), preferred_element_type=F32)


def _bmm_nt(a, b):
    return jnp.einsum("cik,cjk->cij", a.astype(BF16), b.astype(BF16), preferred_element_type=F32)


def _scan_kernel(*refs, cg, n_pairs):
    in_refs = (refs[:6], refs[6:12])
    y_refs = refs[12:14]
    h_s, mc_s, cc_s, hs_s = refs[14:]
    gi = pl.program_id(2)

    @pl.when(gi == 0)
    def _():
        h_s[...] = jnp.zeros_like(h_s)

    two = 2 * CHUNK
    lane = lax.broadcasted_iota(jnp.int32, (1, 1, LANES), 2)
    head0 = lane < HEAD_DIM
    row = lax.broadcasted_iota(jnp.int32, (1, two, two), 1)
    col = lax.broadcasted_iota(jnp.int32, (1, two, two), 2)
    same = (row // CHUNK) == (col // CHUNK)
    ahead = (row % CHUNK) - (col % CHUNK)
    eye = (row == col).astype(F32)
    chains = [(z, pp) for z in range(2) for pp in range(n_pairs)]

    def stacked(ref, pp):
        x = ref[0, :, pp * LANES:(pp + 1) * LANES].reshape(cg, CHUNK, LANES)
        zero = jnp.zeros_like(x)
        return jnp.concatenate([jnp.where(head0, x, zero), jnp.where(head0, zero, x)], axis=1)

    ops = []
    for z, pp in chains:
        kq_ref, rq_ref, kd_ref, bk_ref, v_ref, egl_ref = in_refs[z]
        ops.append(dict(
            kq=stacked(kq_ref.at[0], pp), rq=stacked(rq_ref.at[0], pp), kd=stacked(kd_ref.at[0], pp),
            bk=stacked(bk_ref.at[0], pp), v=stacked(v_ref, pp),
            egl=egl_ref[0, 0, :, :, pp * LANES:(pp + 1) * LANES],
            strict=same & ((ahead > 0) if z == 0 else (ahead < 0)),
            incl=same & ((ahead >= 0) if z == 0 else (ahead <= 0))))

    for o in ops:
        gram = _bmm_nt(jnp.concatenate([o["kq"], o["rq"]], axis=1), jnp.concatenate([o["kd"], o["bk"]], axis=1))
        o["a_kk"] = jnp.where(o["strict"], gram[:, :two, :two], 0.0)
        o["pw"] = -jnp.where(o["strict"], gram[:, :two, two:], 0.0)
        o["a_rk"] = jnp.where(o["incl"], gram[:, two:, :two], 0.0)
        o["a_rb"] = jnp.where(o["incl"], gram[:, two:, two:], 0.0)
        o["tinv"] = eye + o["pw"]

    for o in ops:
        o["pw"] = _bmm(o["pw"], o["pw"])
    for _ in range(4):
        for o in ops:
            both = _bmm(o["pw"], jnp.concatenate([o["pw"].astype(BF16), o["tinv"].astype(BF16)], axis=2))
            o["pw"] = both[:, :, :two]
            o["tinv"] = o["tinv"] + both[:, :, two:]
    for o in ops:
        o["tinv"] = o["tinv"] + _bmm(o["pw"], o["tinv"])
    for o in ops:
        o["akv"] = _bmm(o["a_kk"], o["v"])
    for o in ops:
        o["wu"] = _bmm(o["tinv"], jnp.concatenate([o["kq"], o["akv"].astype(BF16)], axis=2))
    for n, o in enumerate(ops):
        bkg_t = jnp.swapaxes(o["bk"].astype(F32) * o["egl"], 1, 2)
        kdg_t = jnp.swapaxes(o["kd"].astype(F32) * o["egl"], 1, 2)
        bwu = _bmm(bkg_t, o["wu"])
        mc_s[n] = (eye * o["egl"] - bwu[:, :, :LANES]).astype(BF16)
        cc_s[n] = _bmm(kdg_t, o["v"]) - bwu[:, :, LANES:]
    for o in ops:
        rwu = _bmm(o["a_rb"], o["wu"])
        o["q"] = o["rq"].astype(F32) - rwu[:, :, :LANES]
        o["y0"] = _bmm(o["a_rk"], o["v"]) - rwu[:, :, LANES:]

    def body(ci, carry):
        for n, (z, _) in enumerate(chains):
            cidx = ci if z == 0 else cg - 1 - ci
            hb = h_s[n].astype(BF16)
            hs_s[n, cidx] = hb
            h_s[n] = _dot(mc_s[n, cidx], hb) + cc_s[n, cidx]
        return carry

    lax.fori_loop(0, cg, body, 0)
    for n, (z, pp) in enumerate(chains):
        y = _bmm(ops[n]["q"], hs_s[n]) + ops[n]["y0"]
        y = (y[:, :CHUNK, :] + y[:, CHUNK:, :]).reshape(cg * CHUNK, LANES)
        y_refs[z][0, :, pp * LANES:(pp + 1) * LANES] = y.astype(y_refs[z].dtype)


def _rwkv_scan(kq, rq, kd, bk, v, egl, cg=8, n_pairs=4):
    _, B, T, c = kq.shape
    lw = n_pairs * LANES
    tl = cg * CHUNK
    ng = T // tl
    assert T % tl == 0 and c % lw == 0
    n_chain = 2 * n_pairs

    in_specs, args = [], []
    for z in range(2):
        tmap = (lambda g: g) if z == 0 else (lambda g: ng - 1 - g)
        dir_in = pl.BlockSpec((1, 1, tl, lw), lambda b, hp, g, z=z, tmap=tmap: (z, b, tmap(g), hp))
        in_specs += [dir_in] * 4
        in_specs.append(pl.BlockSpec((1, tl, lw), lambda b, hp, g, tmap=tmap: (b, tmap(g), hp)))
        in_specs.append(pl.BlockSpec((1, 1, cg, 1, lw), lambda b, hp, g, z=z, tmap=tmap: (z, b, tmap(g), 0, hp)))
        args += [kq, rq, kd, bk, v, egl]
    out_specs = [pl.BlockSpec((1, tl, lw), lambda b, hp, g: (b, g, hp)),
                 pl.BlockSpec((1, tl, lw), lambda b, hp, g: (b, ng - 1 - g, hp))]
    kernel = functools.partial(_scan_kernel, cg=cg, n_pairs=n_pairs)
    return pl.pallas_call(
        kernel,
        grid=(B, c // lw, ng),
        in_specs=in_specs,
        out_specs=out_specs,
        out_shape=[jax.ShapeDtypeStruct((B, T, c), BF16)] * 2,
        scratch_shapes=[pltpu.VMEM((n_chain, LANES, LANES), F32),
                        pltpu.VMEM((n_chain, cg, LANES, LANES), BF16),
                        pltpu.VMEM((n_chain, cg, LANES, LANES), F32),
                        pltpu.VMEM((n_chain, cg, LANES, LANES), BF16)],
        compiler_params=_cparams(("parallel", "parallel", "arbitrary")),
        name="rwkv_scan",
    )(*args)


def _mix_kernel(x_ref, att_ref, yf_ref, yb_ref, bonus_ref, gate_ref, anw_ref, lnw_ref, lnb_ref, bd_ref, wo_ref,
                n2w_ref, rw_ref, rb_ref, tril_ref, h_ref, hn_ref, idx_ref, gates_ref, rank_ref, cnt_ref, cnt_s):
    @pl.when(pl.program_id(0) == 0)
    def _():
        cnt_s[...] = jnp.zeros_like(cnt_s)

    bd = bd_ref[...]
    inv_n = 1.0 / HEAD_DIM
    att_w = att_ref.shape[-1]
    rows = tril_ref.shape[0]
    lane = lax.broadcasted_iota(jnp.int32, (rows, LANES), 1)
    cnt = cnt_s[...]
    n_sub = x_ref.shape[0] // rows
    all_logits = []
    for j in range(n_sub):
        sl = slice(j * rows, (j + 1) * rows)
        att = att_ref[sl, :].astype(F32)
        ms = jnp.mean(att * att, axis=-1, keepdims=True)
        att_n = att * lax.rsqrt(ms + RMS_EPS) * anw_ref[...]

        y = yf_ref[sl, :].astype(F32) + yb_ref[sl, :].astype(F32)
        mean = _dot_exact_rhs(y, bd) * inv_n
        yc = y - mean
        var = _dot_exact_rhs(yc * yc, bd) * inv_n
        yn = yc * lax.rsqrt(var + GN_EPS) * lnw_ref[...] + lnb_ref[...]
        rk = (yn + bonus_ref[sl, :].astype(F32)) * gate_ref[sl, :].astype(F32)

        mix = _dot(att_n.astype(BF16), wo_ref[:att_w, :]) + _dot(rk.astype(BF16), wo_ref[att_w:, :])
        h = x_ref[sl, :] + mix
        h_ref[sl, :] = h
        ms2 = jnp.mean(h * h, axis=-1, keepdims=True)
        hn = h * lax.rsqrt(ms2 + RMS_EPS) * n2w_ref[...]
        hn_ref[sl, :] = hn

        hn_hi, hn_lo = _split2(hn)
        all_logits.append(_dot(hn_hi, rw_ref[0]) + _dot(hn_lo, rw_ref[0]) + _dot(hn_hi, rw_ref[1]) + rb_ref[...])

    for j, logits in enumerate(all_logits):
        sl = slice(j * rows, (j + 1) * rows)
        cur = logits
        vals, idxs = [], []
        for _ in range(TOP_K):
            m = jnp.max(cur, axis=-1, keepdims=True)
            sel = jnp.min(jnp.where(cur == m, lane, LANES), axis=-1, keepdims=True)
            vals.append(m)
            idxs.append(sel)
            cur = jnp.where(lane == sel, -jnp.inf, cur)
        es = [jnp.exp(vk - vals[0]) for vk in vals]
        tot = es[0] + es[1] + es[2] + es[3]
        onehot = jnp.zeros(logits.shape, F32)
        for kk in range(TOP_K):
            onehot = onehot + (lane == idxs[kk]).astype(F32)
        before = cnt + _dot(tril_ref[...], onehot.astype(BF16))
        cnt = cnt + jnp.sum(onehot, axis=0, keepdims=True)

        idx_out = jnp.zeros(logits.shape, jnp.int32)
        g_out = jnp.zeros(logits.shape, F32)
        rank_out = jnp.zeros(logits.shape, F32)
        for kk in range(TOP_K):
            idx_out = jnp.where(lane == kk, idxs[kk], idx_out)
            g_out = jnp.where(lane == kk, es[kk] / tot, g_out)
            rank_k = jnp.sum(jnp.where(lane == idxs[kk], before, 0.0), axis=-1, keepdims=True)
            rank_out = jnp.where(lane == kk, rank_k, rank_out)
        idx_ref[sl, :] = idx_out
        gates_ref[sl, :] = g_out
        rank_ref[sl, :] = rank_out.astype(jnp.int32)
    cnt_s[...] = cnt
    cnt_ref[...] = cnt


def _mix(x2d, att, y_fwd, y_bwd, bonus, gate, attn_norm_w, ln_w, ln_b, w_o, norm2_w, router_w, router_b,
         tm=512, n_split=2):
    M, D = x2d.shape
    c = y_fwd.shape[-1]
    att_w = att.shape[-1]
    ch = np.arange(c) // HEAD_DIM
    bd = jnp.asarray((ch[:, None] == ch[None, :]).astype(np.float32), BF16)
    t = np.arange(tm // n_split)
    tril = jnp.asarray((t[None, :] < t[:, None]).astype(np.float32), BF16)
    rw = jnp.zeros((D, LANES), F32).at[:, :N_EXPERTS].set(router_w)
    rw = jnp.stack(_split2(rw))
    rb = jnp.full((1, LANES), -jnp.inf, F32).at[0, :N_EXPERTS].set(router_b)
    row = lambda u: u.reshape(1, -1).astype(F32)
    const = lambda shape: pl.BlockSpec(shape, lambda i: (0,) * len(shape))
    tok = lambda w: pl.BlockSpec((tm, w), lambda i: (i, 0))
    return pl.pallas_call(
        _mix_kernel,
        grid=(M // tm,),
        in_specs=[
            tok(D), tok(att_w), tok(c), tok(c), tok(c), tok(c),
            const((1, att_w)), const((1, c)), const((1, c)), const((c, c)), const((att_w + c, D)),
            const((1, D)), const((2, D, LANES)), const((1, LANES)), const((tm // n_split, tm // n_split)),
        ],
        out_specs=[
            tok(D), tok(D),
            tok(LANES), tok(LANES), tok(LANES),
            const((1, LANES)),
        ],
        out_shape=[jax.ShapeDtypeStruct((M, D), F32), jax.ShapeDtypeStruct((M, D), F32),
                   jax.ShapeDtypeStruct((M, LANES), jnp.int32), jax.ShapeDtypeStruct((M, LANES), F32),
                   jax.ShapeDtypeStruct((M, LANES), jnp.int32), jax.ShapeDtypeStruct((1, LANES), F32)],
        scratch_shapes=[pltpu.VMEM((1, LANES), F32)],
        compiler_params=_cparams(("arbitrary",)),
        name="mix_router",
    )(x2d, att, y_fwd, y_bwd, bonus, gate, row(attn_norm_w), row(ln_w), row(ln_b), bd, w_o.astype(BF16),
      row(norm2_w), rw, rb, tril)


def _route(top_idx, rank, counts, blk):
    M = top_idx.shape[0]
    counts = counts.astype(jnp.int32)
    padded = (counts + blk - 1) // blk * blk
    pend = jnp.cumsum(padded)
    pstart = pend - padded
    experts = jnp.arange(N_EXPERTS, dtype=jnp.int32)
    first = jnp.sum(jnp.where(top_idx[:, :, None] == experts, pstart, 0), axis=-1)
    dest = (first + rank).astype(jnp.int32)
    n_blk = (M * TOP_K + N_EXPERTS * blk) // blk
    starts = jnp.arange(n_blk, dtype=jnp.int32) * blk
    blk_expert = jnp.sum((pend[None, :] <= starts[:, None]).astype(jnp.int32), axis=1)
    blk_expert = jnp.minimum(blk_expert, N_EXPERTS - 1).astype(jnp.int32)
    n_used = (pend[-1] // blk).reshape(1).astype(jnp.int32)
    return dest, counts, padded.astype(jnp.int32), pstart.astype(jnp.int32), blk_expert, n_used


def _dispatch_kernel(cnt_ref, pad_ref, first_ref, dest_ref, hn_ref, xb_ref, dest_smem, zero_s, sem_idx, sem, sem_pad,
                     *, tm):
    i = pl.program_id(0)
    cp = pltpu.make_async_copy(dest_ref.at[0], dest_smem, sem_idx)
    cp.start()
    cp.wait()

    def issue(t, carry):
        for kk in range(TOP_K):
            row = dest_smem[0, t * TOP_K + kk]
            pltpu.make_async_copy(hn_ref.at[pl.ds(t, 1)], xb_ref.at[pl.ds(row, 1)], sem).start()
        return carry

    lax.fori_loop(0, tm, issue, 0)

    @pl.when(i == 0)
    def _():
        zero_s[...] = jnp.zeros_like(zero_s)

        def zero_rows(lo, hi):
            def start(r, c2):
                pltpu.make_async_copy(zero_s.at[pl.ds(0, 1)], xb_ref.at[pl.ds(r, 1)], sem_pad).start()
                return c2

            def wait(r, c2):
                pltpu.make_async_copy(zero_s.at[pl.ds(0, 1)], xb_ref.at[pl.ds(r, 1)], sem_pad).wait()
                return c2

            lax.fori_loop(lo, hi, start, 0)
            lax.fori_loop(lo, hi, wait, 0)

        def per_expert(e, carry):
            zero_rows(first_ref[e] + cnt_ref[e], first_ref[e] + pad_ref[e])
            return carry

        lax.fori_loop(0, N_EXPERTS, per_expert, 0)
        zero_rows(first_ref[N_EXPERTS - 1] + pad_ref[N_EXPERTS - 1], xb_ref.shape[0])

    for kk in range(TOP_K):
        pltpu.make_async_copy(hn_ref, xb_ref.at[pl.ds(0, tm)], sem).wait()


def _dispatch(hn, dest, counts, padded, pstart, n_rows, tm=256):
    M, D = hn.shape
    kernel = functools.partial(_dispatch_kernel, tm=tm)
    grid_spec = pltpu.PrefetchScalarGridSpec(
        num_scalar_prefetch=3,
        grid=(M // tm,),
        in_specs=[pl.BlockSpec((1, 1, TOP_K * tm), lambda i, *_: (i, 0, 0)),
                  pl.BlockSpec((tm, D), lambda i, *_: (i, 0))],
        out_specs=pl.BlockSpec(memory_space=pl.ANY),
        scratch_shapes=[pltpu.SMEM((1, TOP_K * tm), jnp.int32), pltpu.VMEM((8, D), F32),
                        pltpu.SemaphoreType.DMA(()), pltpu.SemaphoreType.DMA(()), pltpu.SemaphoreType.DMA(())],
    )
    return pl.pallas_call(
        kernel,
        grid_spec=grid_spec,
        out_shape=jax.ShapeDtypeStruct((n_rows, D), F32),
        compiler_params=_cparams(("arbitrary",)),
        name="dispatch",
    )(counts, padded, pstart, dest.reshape(M // tm, 1, TOP_K * tm), hn)


def _expert_kernel(be_ref, nu_ref, x_ref, wu_ref, bu_ref, wd_ref, bdn_ref, y_ref, wu_s, wd_s, *, f, n_split):
    i = pl.program_id(0)
    e = be_ref[i]
    changed = jnp.logical_or(i == 0, e != be_ref[jnp.maximum(i - 1, 0)])

    @pl.when(changed)
    def _():
        wu_s[...] = wu_ref[0].astype(BF16)
        wd_s[...] = wd_ref[0].astype(BF16)

    @pl.when(i < nu_ref[0])
    def _():
        rows = x_ref.shape[0] // n_split
        hcats = [_dot(x_ref[j * rows:(j + 1) * rows, :].astype(BF16), wu_s[...]) + bu_ref[0] for j in range(n_split)]
        for j, hcat in enumerate(hcats):
            gate = jnp.minimum(hcat[:, :f], SWIGLU_LIMIT)
            lin = jnp.clip(hcat[:, f:], -SWIGLU_LIMIT, SWIGLU_LIMIT)
            act = (lin + 1.0) * (gate * jax.nn.sigmoid(SWIGLU_ALPHA * gate))
            y_ref[j * rows:(j + 1) * rows, :] = _dot(act.astype(BF16), wd_s[...]) + bdn_ref[0]

    @pl.when(i >= nu_ref[0])
    def _():
        y_ref[...] = jnp.zeros_like(y_ref)


def _experts(x_buf, blk_expert, n_used, w_up, b_up, w_down, b_down, blk, n_split=2):
    P, D = x_buf.shape
    E, _, f2 = w_up.shape
    f = f2 // 2
    kernel = functools.partial(_expert_kernel, f=f, n_split=n_split)
    grid_spec = pltpu.PrefetchScalarGridSpec(
        num_scalar_prefetch=2,
        grid=(P // blk,),
        in_specs=[
            pl.BlockSpec((blk, D), lambda i, be, nu: (jnp.minimum(i, nu[0] - 1), 0)),
            pl.BlockSpec((1, D, f2), lambda i, be, nu: (be[i], 0, 0)),
            pl.BlockSpec((1, 1, f2), lambda i, be, nu: (be[i], 0, 0)),
            pl.BlockSpec((1, f, D), lambda i, be, nu: (be[i], 0, 0)),
            pl.BlockSpec((1, 1, D), lambda i, be, nu: (be[i], 0, 0)),
        ],
        out_specs=pl.BlockSpec((blk, D), lambda i, be, nu: (i, 0)),
        scratch_shapes=[pltpu.VMEM((D, f2), BF16), pltpu.VMEM((f, D), BF16)],
    )
    return pl.pallas_call(
        kernel,
        grid_spec=grid_spec,
        out_shape=jax.ShapeDtypeStruct((P, D), F32),
        compiler_params=_cparams(("arbitrary",)),
        name="experts",
    )(blk_expert, n_used, x_buf, w_up, b_up.reshape(E, 1, f2), w_down, b_down.reshape(E, 1, D))


def _combine_kernel(dfirst_ref, dnext_ref, h_ref, g_ref, fw_ref, yb_ref, o_ref, dest_smem, yg_s, sem_idx, sem, *, tm):
    i = pl.program_id(0)
    n = pl.num_programs(0)

    def gather_tile(dref, slot):
        cp = pltpu.make_async_copy(dref.at[0], dest_smem, sem_idx)
        cp.start()
        cp.wait()

        def issue(t, carry):
            for kk in range(TOP_K):
                row = dest_smem[0, t * TOP_K + kk]
                pltpu.make_async_copy(yb_ref.at[pl.ds(row, 1)], yg_s.at[slot, kk, pl.ds(t, 1)], sem.at[slot]).start()
            return carry

        lax.fori_loop(0, tm, issue, 0)

    @pl.when(i == 0)
    def _():
        gather_tile(dfirst_ref, 0)

    @pl.when(i + 1 < n)
    def _():
        gather_tile(dnext_ref, (i + 1) % 2)

    slot = i % 2
    g = g_ref[...]
    acc = h_ref[...]
    for kk in range(TOP_K):
        pltpu.make_async_copy(yb_ref.at[pl.ds(0, tm)], yg_s.at[slot, kk], sem.at[slot]).wait()
    for kk in range(TOP_K):
        acc = acc + g[:, kk:kk + 1] * yg_s[slot, kk]
    ms = jnp.mean(acc * acc, axis=-1, keepdims=True)
    o_ref[...] = acc * lax.rsqrt(ms + RMS_EPS) * fw_ref[...]


def _combine(h, y_buf, dest, gates, final_w, tm=256):
    M, D = h.shape
    kernel = functools.partial(_combine_kernel, tm=tm)
    n = M // tm
    dest_tiles = dest.reshape(n, 1, TOP_K * tm)
    return pl.pallas_call(
        kernel,
        grid=(n,),
        in_specs=[
            pl.BlockSpec((1, 1, TOP_K * tm), lambda i: (0, 0, 0)),
            pl.BlockSpec((1, 1, TOP_K * tm), lambda i: (jnp.minimum(i + 1, n - 1), 0, 0)),
            pl.BlockSpec((tm, D), lambda i: (i, 0)),
            pl.BlockSpec((tm, LANES), lambda i: (i, 0)),
            pl.BlockSpec((1, D), lambda i: (0, 0)),
            pl.BlockSpec(memory_space=pl.ANY),
        ],
        out_specs=pl.BlockSpec((tm, D), lambda i: (i, 0)),
        out_shape=jax.ShapeDtypeStruct((M, D), F32),
        scratch_shapes=[pltpu.SMEM((1, TOP_K * tm), jnp.int32), pltpu.VMEM((2, TOP_K, tm, D), F32),
                        pltpu.SemaphoreType.DMA(()), pltpu.SemaphoreType.DMA((2,))],
        compiler_params=_cparams(("arbitrary",)),
        name="combine",
    )(dest_tiles, dest_tiles, h, gates, final_w.reshape(1, D).astype(F32), y_buf)


def kernel(x, norm1_w, w_in, attn_rpb, attn_norm_w, rwkv_mu_prev, rwkv_mu_next, rwkv_w0, rwkv_w2, rwkv_a0, rwkv_a2, rwkv_g2, rwkv_k_k, rwkv_k_a, rwkv_r_k, rwkv_ln_w, rwkv_ln_b, w_o, norm2_w, router_w, router_b, expert_w_up, expert_b_up, expert_w_down, expert_b_down, final_norm_w):
    B, T, D = x.shape
    M = B * T
    depth = norm1_w.shape[0]
    assert depth == 1, "the combine stage folds in the final norm, so it must follow the only layer"
    c = rwkv_k_k.shape[-1]
    att_w = attn_norm_w.shape[-1]
    blk = 512
    h = x.reshape(M, D)
    for l in range(depth):
        qkv, p = _in_proj(h, norm1_w[l], w_in[l].astype(BF16), 3 * att_w)
        att = _na2d(qkv.reshape(B, T, 3 * att_w), _na_bias_table(attn_rpb[l]), att_w)
        kq, rq, kd, bk, egl, v, bonus, gate = _rwkv_prep(
            p.reshape(B, T, -1), rwkv_mu_prev[l], rwkv_mu_next[l], rwkv_w0[l].reshape(-1), rwkv_w2[l],
            rwkv_a0[l].reshape(-1), rwkv_a2[l], rwkv_g2[l], rwkv_k_k[l], rwkv_k_a[l], rwkv_r_k[l].reshape(-1))
        y_fwd, y_bwd = _rwkv_scan(kq, rq, kd, bk, v, egl)
        h, hn, idx_pad, gates_pad, rank_pad, counts = _mix(
            h, att.reshape(M, att_w), y_fwd.reshape(M, c), y_bwd.reshape(M, c), bonus.reshape(M, c),
            gate.reshape(M, c),
            attn_norm_w[l], rwkv_ln_w[l], rwkv_ln_b[l], w_o[l], norm2_w[l], router_w[l], router_b[l])
        dest, cnt, padded, pstart, blk_expert, n_used = _route(
            idx_pad[:, :TOP_K], rank_pad[:, :TOP_K], counts[0, :N_EXPERTS], blk)
        n_rows = (M * TOP_K + N_EXPERTS * blk) // blk * blk
        x_buf = _dispatch(hn, dest, cnt, padded, pstart, n_rows)
        y_buf = _experts(x_buf, blk_expert, n_used, expert_w_up[l], expert_b_up[l], expert_w_down[l],
                         expert_b_down[l], blk)
        h = _combine(h, y_buf, dest, gates_pad, final_norm_w)
    return h.reshape(B, T, D)
```

```python
import functools

import numpy as np
import jax
import jax.numpy as jnp
from jax import lax
from jax.experimental import pallas as pl
from jax.experimental.pallas import tpu as pltpu

F32 = jnp.float32
BF16 = jnp.bfloat16

HEAD_DIM = 64
GRID_W = 64
WIN_H = 8
WIN_W = 16
N_EXPERTS = 32
TOP_K = 4
SWIGLU_ALPHA = 1.702
SWIGLU_LIMIT = 7.0
RMS_EPS = 1e-5
GN_EPS = 64e-5
D_LORA = 64
D_GATE_LORA = 128
CHUNK = 64
LANES = 128
SUBLANES = 8
MASK_NEG = -1e30
VMEM_LIMIT = 56 * 1024 * 1024


def _cparams(sem):
    return pltpu.CompilerParams(dimension_semantics=sem, vmem_limit_bytes=VMEM_LIMIT)


def _dot(a, b):
    return jnp.dot(a, b, preferred_element_type=F32)


def _split2(x):
    hi = x.astype(BF16)
    lo = (x - hi.astype(F32)).astype(BF16)
    return hi, lo


def _dot_exact_lhs(a01, x):
    hi, lo = _split2(x)
    return _dot(a01, hi) + _dot(a01, lo)


def _dot_exact_rhs(x, b01):
    hi, lo = _split2(x)
    return _dot(hi, b01) + _dot(lo, b01)


def _inproj_kernel(x_ref, nw_ref, w_ref, qkv_ref, p_ref):
    x = x_ref[...]
    ms = jnp.mean(x * x, axis=-1, keepdims=True)
    hn = (x * lax.rsqrt(ms + RMS_EPS) * nw_ref[...]).astype(BF16)
    n_qkv = qkv_ref.shape[-1]
    qkv_ref[...] = _dot(hn, w_ref[:, :n_qkv]).astype(BF16)
    p_ref[...] = _dot(hn, w_ref[:, n_qkv:]).astype(BF16)


def _in_proj(x2d, norm_w, w_in_bf16, n_qkv, tm=512):
    M, D = x2d.shape
    n_all = w_in_bf16.shape[1]
    n_p = n_all - n_qkv
    return pl.pallas_call(
        _inproj_kernel,
        grid=(M // tm,),
        in_specs=[
            pl.BlockSpec((tm, D), lambda i: (i, 0)),
            pl.BlockSpec((1, D), lambda i: (0, 0)),
            pl.BlockSpec((D, n_all), lambda i: (0, 0)),
        ],
        out_specs=[
            pl.BlockSpec((tm, n_qkv), lambda i: (i, 0)),
            pl.BlockSpec((tm, n_p), lambda i: (i, 0)),
        ],
        out_shape=[jax.ShapeDtypeStruct((M, n_qkv), BF16), jax.ShapeDtypeStruct((M, n_p), BF16)],
        compiler_params=_cparams(("parallel",)),
        name="in_proj",
    )(x2d, norm_w.reshape(1, D), w_in_bf16)


def _na_bias_table(rpb):
    H = rpb.shape[0]
    cols = np.arange(GRID_W)
    col_start = np.clip(cols - WIN_W // 2, 0, GRID_W - WIN_W)
    kc = np.arange(GRID_W)
    valid = (kc[None, :] >= col_start[:, None]) & (kc[None, :] < col_start[:, None] + WIN_W)
    dc = kc[None, :] - cols[:, None] + (WIN_W - 1)
    var = np.arange(WIN_H)
    wi = np.arange(WIN_H)
    dr = wi[None, :] - var[:, None] + (WIN_H - 1)
    row_sel = (dr[:, :, None] == np.arange(2 * WIN_H - 1)).astype(np.float32)
    col_sel = ((dc[:, :, None] == np.arange(2 * WIN_W - 1)) & valid[:, :, None]).astype(np.float32)
    tab = jnp.einsum("vir,hrd,xyd->vhxiy", row_sel, rpb.astype(F32), col_sel, precision=lax.Precision.HIGHEST)
    tab = jnp.where(valid[None, None, :, None, :], tab, MASK_NEG)
    return tab.reshape(WIN_H, H, GRID_W, WIN_H * GRID_W)


def _na_kernel(q_ref, k_ref, v_ref, bias_ref, o_ref, *, rows_per_step, n_rows):
    i = pl.program_id(2)
    lane = lax.broadcasted_iota(jnp.int32, (1, LANES), 1)
    head0 = lane < HEAD_DIM
    scale = HEAD_DIM ** -0.5
    nkeys = WIN_H * GRID_W

    def scores(rr):
        r = i * rows_per_step + rr
        r_start = jnp.clip(r - WIN_H // 2, 0, n_rows - WIN_H)
        var = r - r_start
        q = q_ref[0, rr * GRID_W:(rr + 1) * GRID_W, :]
        koff = pl.multiple_of(r_start * GRID_W, GRID_W)
        kwin = k_ref[0, pl.ds(koff, nkeys), :]
        zero = jnp.zeros_like(q)
        q2 = jnp.concatenate([jnp.where(head0, q, zero), jnp.where(head0, zero, q)], axis=0)
        s = lax.dot_general(q2, kwin, (((1,), (1,)), ((), ())), preferred_element_type=F32)
        return koff, s * scale + bias_ref[var].reshape(2 * GRID_W, nkeys)

    def finish(rr, koff, s):
        vwin = v_ref[0, pl.ds(koff, nkeys), :]
        m = jnp.max(s, axis=-1, keepdims=True)
        e = jnp.exp(s - m)
        denom = jnp.sum(e, axis=-1, keepdims=True)
        o = _dot(e.astype(BF16), vwin) / denom
        o_ref[0, rr * GRID_W:(rr + 1) * GRID_W, :] = jnp.where(head0, o[:GRID_W], o[GRID_W:]).astype(o_ref.dtype)

    ahead = 3
    pending = [scores(rr) for rr in range(min(ahead, rows_per_step))]
    for rr in range(rows_per_step):
        if rr + ahead < rows_per_step:
            pending.append(scores(rr + ahead))
        finish(rr, *pending.pop(0))


def _na2d(qkv, bias_tab, att_w, rows_per_step=16):
    B, T, _ = qkv.shape
    n_rows = T // GRID_W
    assert n_rows >= WIN_H and n_rows % rows_per_step == 0
    n_pairs = att_w // LANES
    tq = rows_per_step * GRID_W
    kernel = functools.partial(_na_kernel, rows_per_step=rows_per_step, n_rows=n_rows)
    return pl.pallas_call(
        kernel,
        grid=(B, n_pairs, n_rows // rows_per_step),
        in_specs=[
            pl.BlockSpec((1, tq, LANES), lambda b, hp, i: (b, i, hp)),
            pl.BlockSpec((1, T, LANES), lambda b, hp, i: (b, 0, n_pairs + hp)),
            pl.BlockSpec((1, T, LANES), lambda b, hp, i: (b, 0, 2 * n_pairs + hp)),
            pl.BlockSpec((WIN_H, 2, GRID_W, WIN_H * GRID_W), lambda b, hp, i: (0, hp, 0, 0)),
        ],
        out_specs=pl.BlockSpec((1, tq, LANES), lambda b, hp, i: (b, i, hp)),
        out_shape=jax.ShapeDtypeStruct((B, T, att_w), BF16),
        compiler_params=_cparams(("parallel", "parallel", "arbitrary")),
        name="na2d",
    )(qkv, qkv, qkv, bias_tab)


def _prep_kernel(p_ref, pprev_ref, pnext_ref, mup_ref, mun_ref, w0_ref, w2_ref, a0_ref, a2_ref, g2_ref,
                 kk_ref, ka_ref, rk_ref, bd_ref, tri_ref,
                 kq_ref, rq_ref, kd_ref, bk_ref, egl_ref, v_ref, bonus_ref, gate_ref, *, tb, c):
    i = pl.program_id(1)
    nb = pl.num_programs(1)
    p = p_ref[0].astype(F32)
    halo = pprev_ref.shape[1]
    prow = jnp.where(i > 0, pprev_ref[0, halo - 1:halo, :].astype(F32), 0.0)
    nrow = jnp.where(i < nb - 1, pnext_ref[0, 0:1, :].astype(F32), 0.0)
    ridx = lax.broadcasted_iota(jnp.int32, (tb, 1), 0)
    p_prev = jnp.where(ridx == 0, prow, pltpu.roll(p, 1, 0))
    p_next = jnp.where(ridx == tb - 1, nrow, pltpu.roll(p, tb - 1, 0))
    pm = p + mup_ref[...] * (p_prev - p) + mun_ref[...] * (p_next - p)
    r = pm[:, :c]
    k = pm[:, c:2 * c]
    v = pm[:, 2 * c:3 * c]
    o = 3 * c
    dw = pm[:, o:o + 2 * D_LORA]
    da = pm[:, o + 2 * D_LORA:o + 4 * D_LORA]
    dg = pm[:, o + 4 * D_LORA:o + 4 * D_LORA + D_GATE_LORA]

    wl = w0_ref[...] + _dot(jnp.tanh(dw).astype(BF16), w2_ref[...])
    neg = -wl
    softplus = jnp.maximum(neg, 0.0) + jnp.log(1.0 + jnp.exp(-jnp.abs(neg)))
    logw = -jnp.exp(-softplus - 0.5)
    a = jax.nn.sigmoid(a0_ref[...] + _dot(da.astype(BF16), a2_ref[...]))
    gate = _dot(jax.nn.sigmoid(dg).astype(BF16), g2_ref[...])

    bd = bd_ref[...]
    kkr = k * kk_ref[...]
    ssq = _dot_exact_rhs(kkr * kkr, bd)
    kk = kkr / jnp.maximum(jnp.sqrt(ssq), 1e-12)

    ksum = jnp.zeros_like(k)
    nchunk = tb // CHUNK
    for z in range(2):
        a_z = a[:, z * c:(z + 1) * c]
        lw = logw[:, z * c:(z + 1) * c]
        kdir = k * (1.0 + (a_z - 1.0) * ka_ref[...])
        ksum = ksum + kdir
        b_z = a_z * kk
        g_incl = _dot_exact_lhs(tri_ref[z], lw)
        last = CHUNK - 1 if z == 0 else 0
        g_all = g_incl.reshape(nchunk, CHUNK, c)[:, last:last + 1, :]
        e_neg = jnp.exp(-g_incl)
        kq_ref[z, 0] = (kk * jnp.exp(g_incl - lw)).astype(BF16)
        rq_ref[z, 0] = (r * jnp.exp(g_incl)).astype(BF16)
        kd_ref[z, 0] = (kdir * e_neg).astype(BF16)
        bk_ref[z, 0] = (b_z * e_neg).astype(BF16)
        egl_ref[z, 0] = jnp.exp(g_all)
    coef = _dot_exact_rhs(r * ksum * rk_ref[...], bd)
    v_ref[0] = v.astype(BF16)
    bonus_ref[0] = (coef * v).astype(BF16)
    gate_ref[0] = gate.astype(BF16)


def _rwkv_prep(p, mu_prev, mu_next, w0, w2, a0, a2, g2, k_k, k_a, r_k, tb=256):
    B, T, pw = p.shape
    c = k_k.shape[0]
    assert T % tb == 0 and tb % CHUNK == 0
    halo = 16
    nhb = T // halo
    nchunk = tb // CHUNK
    w2bd = jnp.zeros((2 * D_LORA, 2 * c), F32)
    a2bd = jnp.zeros((2 * D_LORA, 2 * c), F32)
    for z in range(2):
        w2bd = w2bd.at[z * D_LORA:(z + 1) * D_LORA, z * c:(z + 1) * c].set(w2[z])
        a2bd = a2bd.at[z * D_LORA:(z + 1) * D_LORA, z * c:(z + 1) * c].set(a2[z])
    ch = np.arange(c) // HEAD_DIM
    bd = jnp.asarray((ch[:, None] == ch[None, :]).astype(np.float32), BF16)
    t = np.arange(tb)
    same = (t[:, None] // CHUNK) == (t[None, :] // CHUNK)
    tri = np.stack([same & (t[None, :] <= t[:, None]), same & (t[None, :] >= t[:, None])]).astype(np.float32)
    tri = jnp.asarray(tri, BF16)
    row = lambda u: u.reshape(1, -1).astype(F32)
    const = lambda shape: pl.BlockSpec(shape, lambda b, i: (0,) * len(shape))
    dir_out = pl.BlockSpec((2, 1, tb, c), lambda b, i: (0, b, i, 0))
    tok_out = pl.BlockSpec((1, tb, c), lambda b, i: (b, i, 0))
    kernel = functools.partial(_prep_kernel, tb=tb, c=c)
    return pl.pallas_call(
        kernel,
        grid=(B, T // tb),
        in_specs=[
            pl.BlockSpec((1, tb, pw), lambda b, i: (b, i, 0)),
            pl.BlockSpec((1, halo, pw), lambda b, i: (b, jnp.maximum(i * (tb // halo) - 1, 0), 0)),
            pl.BlockSpec((1, halo, pw), lambda b, i: (b, jnp.minimum((i + 1) * (tb // halo), nhb - 1), 0)),
            const((1, pw)), const((1, pw)),
            const((1, 2 * c)), const((2 * D_LORA, 2 * c)),
            const((1, 2 * c)), const((2 * D_LORA, 2 * c)),
            const((D_GATE_LORA, c)),
            const((1, c)), const((1, c)), const((1, c)),
            const((c, c)), const((2, tb, tb)),
        ],
        out_specs=[dir_out, dir_out, dir_out, dir_out,
                   pl.BlockSpec((2, 1, nchunk, 1, c), lambda b, i: (0, b, i, 0, 0)),
                   tok_out, tok_out, tok_out],
        out_shape=[jax.ShapeDtypeStruct((2, B, T, c), BF16)] * 4
        + [jax.ShapeDtypeStruct((2, B, T // CHUNK, 1, c), F32)]
        + [jax.ShapeDtypeStruct((B, T, c), BF16)] * 3,
        compiler_params=_cparams(("parallel", "parallel")),
        name="rwkv_prep",
    )(p, p, p, row(mu_prev), row(mu_next), row(w0), w2bd.astype(BF16), row(a0), a2bd.astype(BF16),
      g2.astype(BF16), row(k_k), row(k_a), row(r_k), bd, tri)


def _bmm(a, b):
    return jnp.einsum("cij,cjk->cik", a.astype(BF16), b.astype(BF16), preferred_element_type=F32)


def _bmm_nt(a, b):
    return jnp.einsum("cik,cjk->cij", a.astype(BF16), b.astype(BF16), preferred_element_type=F32)


def _scan_kernel(*refs, cg, n_pairs):
    in_refs = (refs[:6], refs[6:12])
    y_refs = refs[12:14]
    h_s, mc_s, cc_s, hs_s = refs[14:]
    gi = pl.program_id(2)

    @pl.when(gi == 0)
    def _():
        h_s[...] = jnp.zeros_like(h_s)

    two = 2 * CHUNK
    lane = lax.broadcasted_iota(jnp.int32, (1, 1, LANES), 2)
    head0 = lane < HEAD_DIM
    row = lax.broadcasted_iota(jnp.int32, (1, two, two), 1)
    col = lax.broadcasted_iota(jnp.int32, (1, two, two), 2)
    same = (row // CHUNK) == (col // CHUNK)
    ahead = (row % CHUNK) - (col % CHUNK)
    eye = (row == col).astype(F32)
    chains = [(z, pp) for z in range(2) for pp in range(n_pairs)]

    def stacked(ref, pp):
        x = ref[0, :, pp * LANES:(pp + 1) * LANES].reshape(cg, CHUNK, LANES)
        zero = jnp.zeros_like(x)
        return jnp.concatenate([jnp.where(head0, x, zero), jnp.where(head0, zero, x)], axis=1)

    ops = []
    for z, pp in chains:
        kq_ref, rq_ref, kd_ref, bk_ref, v_ref, egl_ref = in_refs[z]
        ops.append(dict(
            kq=stacked(kq_ref.at[0], pp), rq=stacked(rq_ref.at[0], pp), kd=stacked(kd_ref.at[0], pp),
            bk=stacked(bk_ref.at[0], pp), v=stacked(v_ref, pp),
            egl=egl_ref[0, 0, :, :, pp * LANES:(pp + 1) * LANES],
            strict=same & ((ahead > 0) if z == 0 else (ahead < 0)),
            incl=same & ((ahead >= 0) if z == 0 else (ahead <= 0))))

    for o in ops:
        gram = _bmm_nt(jnp.concatenate([o["kq"], o["rq"]], axis=1), jnp.concatenate([o["kd"], o["bk"]], axis=1))
        o["a_kk"] = jnp.where(o["strict"], gram[:, :two, :two], 0.0)
        o["pw"] = -jnp.where(o["strict"], gram[:, :two, two:], 0.0)
        o["a_rk"] = jnp.where(o["incl"], gram[:, two:, :two], 0.0)
        o["a_rb"] = jnp.where(o["incl"], gram[:, two:, two:], 0.0)
        o["tinv"] = eye + o["pw"]

    for o in ops:
        o["pw"] = _bmm(o["pw"], o["pw"])
    for _ in range(4):
        for o in ops:
            both = _bmm(o["pw"], jnp.concatenate([o["pw"].astype(BF16), o["tinv"].astype(BF16)], axis=2))
            o["pw"] = both[:, :, :two]
            o["tinv"] = o["tinv"] + both[:, :, two:]
    for o in ops:
        o["tinv"] = o["tinv"] + _bmm(o["pw"], o["tinv"])
    for o in ops:
        o["akv"] = _bmm(o["a_kk"], o["v"])
    for o in ops:
        o["wu"] = _bmm(o["tinv"], jnp.concatenate([o["kq"], o["akv"].astype(BF16)], axis=2))
    for n, o in enumerate(ops):
        bkg_t = jnp.swapaxes(o["bk"].astype(F32) * o["egl"], 1, 2)
        kdg_t = jnp.swapaxes(o["kd"].astype(F32) * o["egl"], 1, 2)
        bwu = _bmm(bkg_t, o["wu"])
        mc_s[n] = (eye * o["egl"] - bwu[:, :, :LANES]).astype(BF16)
        cc_s[n] = _bmm(kdg_t, o["v"]) - bwu[:, :, LANES:]
    for o in ops:
        rwu = _bmm(o["a_rb"], o["wu"])
        o["q"] = o["rq"].astype(F32) - rwu[:, :, :LANES]
        o["y0"] = _bmm(o["a_rk"], o["v"]) - rwu[:, :, LANES:]

    def body(ci, carry):
        for n, (z, _) in enumerate(chains):
            cidx = ci if z == 0 else cg - 1 - ci
            hb = h_s[n].astype(BF16)
            hs_s[n, cidx] = hb
            h_s[n] = _dot(mc_s[n, cidx], hb) + cc_s[n, cidx]
        return carry

    lax.fori_loop(0, cg, body, 0)
    for n, (z, pp) in enumerate(chains):
        y = _bmm(ops[n]["q"], hs_s[n]) + ops[n]["y0"]
        y = (y[:, :CHUNK, :] + y[:, CHUNK:, :]).reshape(cg * CHUNK, LANES)
        y_refs[z][0, :, pp * LANES:(pp + 1) * LANES] = y.astype(y_refs[z].dtype)


def _rwkv_scan(kq, rq, kd, bk, v, egl, cg=8, n_pairs=4):
    _, B, T, c = kq.shape
    lw = n_pairs * LANES
    tl = cg * CHUNK
    ng = T // tl
    assert T % tl == 0 and c % lw == 0
    n_chain = 2 * n_pairs

    in_specs, args = [], []
    for z in range(2):
        tmap = (lambda g: g) if z == 0 else (lambda g: ng - 1 - g)
        dir_in = pl.BlockSpec((1, 1, tl, lw), lambda b, hp, g, z=z, tmap=tmap: (z, b, tmap(g), hp))
        in_specs += [dir_in] * 4
        in_specs.append(pl.BlockSpec((1, tl, lw), lambda b, hp, g, tmap=tmap: (b, tmap(g), hp)))
        in_specs.append(pl.BlockSpec((1, 1, cg, 1, lw), lambda b, hp, g, z=z, tmap=tmap: (z, b, tmap(g), 0, hp)))
        args += [kq, rq, kd, bk, v, egl]
    out_specs = [pl.BlockSpec((1, tl, lw), lambda b, hp, g: (b, g, hp)),
                 pl.BlockSpec((1, tl, lw), lambda b, hp, g: (b, ng - 1 - g, hp))]
    kernel = functools.partial(_scan_kernel, cg=cg, n_pairs=n_pairs)
    return pl.pallas_call(
        kernel,
        grid=(B, c // lw, ng),
        in_specs=in_specs,
        out_specs=out_specs,
        out_shape=[jax.ShapeDtypeStruct((B, T, c), BF16)] * 2,
        scratch_shapes=[pltpu.VMEM((n_chain, LANES, LANES), F32),
                        pltpu.VMEM((n_chain, cg, LANES, LANES), BF16),
                        pltpu.VMEM((n_chain, cg, LANES, LANES), F32),
                        pltpu.VMEM((n_chain, cg, LANES, LANES), BF16)],
        compiler_params=_cparams(("parallel", "parallel", "arbitrary")),
        name="rwkv_scan",
    )(*args)


def _mix_kernel(x_ref, att_ref, yf_ref, yb_ref, bonus_ref, gate_ref, anw_ref, lnw_ref, lnb_ref, bd_ref, wo_ref,
                n2w_ref, rw_ref, rb_ref, tril_ref, h_ref, hn_ref, idx_ref, gates_ref, rank_ref, cnt_ref, cnt_s):
    @pl.when(pl.program_id(0) == 0)
    def _():
        cnt_s[...] = jnp.zeros_like(cnt_s)

    bd = bd_ref[...]
    inv_n = 1.0 / HEAD_DIM
    att_w = att_ref.shape[-1]
    rows = tril_ref.shape[0]
    lane = lax.broadcasted_iota(jnp.int32, (rows, LANES), 1)
    cnt = cnt_s[...]
    n_sub = x_ref.shape[0] // rows
    subs = [slice(j * rows, (j + 1) * rows) for j in range(n_sub)]
    ys = [yf_ref[sl, :].astype(F32) + yb_ref[sl, :].astype(F32) for sl in subs]
    ycs = [y - _dot_exact_rhs(y, bd) * inv_n for y in ys]
    vrs = [_dot_exact_rhs(yc * yc, bd) * inv_n for yc in ycs]
    mixes = []
    for sl, yc, var in zip(subs, ycs, vrs):
        att = att_ref[sl, :].astype(F32)
        ms = jnp.mean(att * att, axis=-1, keepdims=True)
        att_n = att * lax.rsqrt(ms + RMS_EPS) * anw_ref[...]
        yn = yc * lax.rsqrt(var + GN_EPS) * lnw_ref[...] + lnb_ref[...]
        rk = (yn + bonus_ref[sl, :].astype(F32)) * gate_ref[sl, :].astype(F32)
        mixes.append(_dot(att_n.astype(BF16), wo_ref[:att_w, :]) + _dot(rk.astype(BF16), wo_ref[att_w:, :]))
    all_logits = []
    for sl, mix in zip(subs, mixes):
        h = x_ref[sl, :] + mix
        h_ref[sl, :] = h
        ms2 = jnp.mean(h * h, axis=-1, keepdims=True)
        hn = h * lax.rsqrt(ms2 + RMS_EPS) * n2w_ref[...]
        hn_ref[sl, :] = hn
        hn_hi, hn_lo = _split2(hn)
        all_logits.append(_dot(hn_hi, rw_ref[0]) + _dot(hn_lo, rw_ref[0]) + _dot(hn_hi, rw_ref[1]) + rb_ref[...])

    for j, logits in enumerate(all_logits):
        sl = slice(j * rows, (j + 1) * rows)
        cur = logits
        vals, idxs = [], []
        for _ in range(TOP_K):
            m = jnp.max(cur, axis=-1, keepdims=True)
            sel = jnp.min(jnp.where(cur == m, lane, LANES), axis=-1, keepdims=True)
            vals.append(m)
            idxs.append(sel)
            cur = jnp.where(lane == sel, -jnp.inf, cur)
        es = [jnp.exp(vk - vals[0]) for vk in vals]
        tot = es[0] + es[1] + es[2] + es[3]
        onehot = jnp.zeros(logits.shape, F32)
        for kk in range(TOP_K):
            onehot = onehot + (lane == idxs[kk]).astype(F32)
        before = cnt + _dot(tril_ref[...], onehot.astype(BF16))
        cnt = cnt + jnp.sum(onehot, axis=0, keepdims=True)

        idx_out = jnp.zeros(logits.shape, jnp.int32)
        g_out = jnp.zeros(logits.shape, F32)
        rank_out = jnp.zeros(logits.shape, F32)
        for kk in range(TOP_K):
            idx_out = jnp.where(lane == kk, idxs[kk], idx_out)
            g_out = jnp.where(lane == kk, es[kk] / tot, g_out)
            rank_k = jnp.sum(jnp.where(lane == idxs[kk], before, 0.0), axis=-1, keepdims=True)
            rank_out = jnp.where(lane == kk, rank_k, rank_out)
        idx_ref[sl, :] = idx_out
        gates_ref[sl, :] = g_out
        rank_ref[sl, :] = rank_out.astype(jnp.int32)
    cnt_s[...] = cnt
    cnt_ref[...] = cnt


def _mix(x2d, att, y_fwd, y_bwd, bonus, gate, attn_norm_w, ln_w, ln_b, w_o, norm2_w, router_w, router_b,
         tm=1024, n_split=4):
    M, D = x2d.shape
    c = y_fwd.shape[-1]
    att_w = att.shape[-1]
    ch = np.arange(c) // HEAD_DIM
    bd = jnp.asarray((ch[:, None] == ch[None, :]).astype(np.float32), BF16)
    t = np.arange(tm // n_split)
    tril = jnp.asarray((t[None, :] < t[:, None]).astype(np.float32), BF16)
    rw = jnp.zeros((D, LANES), F32).at[:, :N_EXPERTS].set(router_w)
    rw = jnp.stack(_split2(rw))
    rb = jnp.full((1, LANES), -jnp.inf, F32).at[0, :N_EXPERTS].set(router_b)
    row = lambda u: u.reshape(1, -1).astype(F32)
    const = lambda shape: pl.BlockSpec(shape, lambda i: (0,) * len(shape))
    tok = lambda w: pl.BlockSpec((tm, w), lambda i: (i, 0))
    return pl.pallas_call(
        _mix_kernel,
        grid=(M // tm,),
        in_specs=[
            tok(D), tok(att_w), tok(c), tok(c), tok(c), tok(c),
            const((1, att_w)), const((1, c)), const((1, c)), const((c, c)), const((att_w + c, D)),
            const((1, D)), const((2, D, LANES)), const((1, LANES)), const((tm // n_split, tm // n_split)),
        ],
        out_specs=[
            tok(D), tok(D),
            tok(LANES), tok(LANES), tok(LANES),
            const((1, LANES)),
        ],
        out_shape=[jax.ShapeDtypeStruct((M, D), F32), jax.ShapeDtypeStruct((M, D), F32),
                   jax.ShapeDtypeStruct((M, LANES), jnp.int32), jax.ShapeDtypeStruct((M, LANES), F32),
                   jax.ShapeDtypeStruct((M, LANES), jnp.int32), jax.ShapeDtypeStruct((1, LANES), F32)],
        scratch_shapes=[pltpu.VMEM((1, LANES), F32)],
        compiler_params=_cparams(("arbitrary",)),
        name="mix_router",
    )(x2d, att, y_fwd, y_bwd, bonus, gate, row(attn_norm_w), row(ln_w), row(ln_b), bd, w_o.astype(BF16),
      row(norm2_w), rw, rb, tril)


def _route(top_idx, rank, counts, blk):
    M = top_idx.shape[0]
    counts = counts.astype(jnp.int32)
    padded = (counts + blk - 1) // blk * blk
    pend = jnp.cumsum(padded)
    pstart = pend - padded
    experts = jnp.arange(N_EXPERTS, dtype=jnp.int32)
    first = jnp.sum(jnp.where(top_idx[:, :, None] == experts, pstart, 0), axis=-1)
    dest = (first + rank).astype(jnp.int32)
    n_blk = (M * TOP_K + N_EXPERTS * blk) // blk
    starts = jnp.arange(n_blk, dtype=jnp.int32) * blk
    blk_expert = jnp.sum((pend[None, :] <= starts[:, None]).astype(jnp.int32), axis=1)
    blk_expert = jnp.minimum(blk_expert, N_EXPERTS - 1).astype(jnp.int32)
    n_used = (pend[-1] // blk).reshape(1).astype(jnp.int32)
    return dest, counts, padded.astype(jnp.int32), pstart.astype(jnp.int32), blk_expert, n_used


def _dispatch_kernel(cnt_ref, pad_ref, first_ref, dest_ref, hn_ref, xb_ref, dest_smem, zero_s, sem_idx, sem, sem_pad,
                     *, tm):
    i = pl.program_id(0)
    cp = pltpu.make_async_copy(dest_ref.at[0], dest_smem, sem_idx)
    cp.start()
    cp.wait()

    def issue(t, carry):
        for kk in range(TOP_K):
            row = dest_smem[0, t * TOP_K + kk]
            pltpu.make_async_copy(hn_ref.at[pl.ds(t, 1)], xb_ref.at[pl.ds(row, 1)], sem).start()
        return carry

    lax.fori_loop(0, tm, issue, 0)

    @pl.when(i == 0)
    def _():
        zero_s[...] = jnp.zeros_like(zero_s)
        zrows = zero_s.shape[0]

        def zero_copy(pos, size):
            return pltpu.make_async_copy(zero_s.at[pl.ds(0, size)], xb_ref.at[pl.ds(pos, size)], sem_pad)

        def padding(e, act):
            pos = first_ref[e] + cnt_ref[e]
            length = pad_ref[e] - cnt_ref[e]
            end = first_ref[e] + pad_ref[e]
            size = zrows
            while size >= SUBLANES:
                has = (length & size) != 0
                end = end - jnp.where(has, size, 0)
                pl.when(has)(functools.partial(act, pl.multiple_of(end, SUBLANES), size))
                size //= 2
            for r in range(SUBLANES - 1):
                pl.when(r < (length & (SUBLANES - 1)))(functools.partial(act, pos + r, 1))

        def start_e(e, carry):
            padding(e, lambda pos, size: zero_copy(pos, size).start())
            return carry

        def wait_e(e, carry):
            padding(e, lambda pos, size: zero_copy(pos, size).wait())
            return carry

        lax.fori_loop(0, N_EXPERTS, start_e, 0)
        lax.fori_loop(0, N_EXPERTS, wait_e, 0)
        tail = first_ref[N_EXPERTS - 1] + pad_ref[N_EXPERTS - 1]
        n_tail = (xb_ref.shape[0] - tail) // zrows

        def start_t(j, carry):
            zero_copy(pl.multiple_of(tail + j * zrows, SUBLANES), zrows).start()
            return carry

        def wait_t(j, carry):
            zero_copy(pl.multiple_of(tail + j * zrows, SUBLANES), zrows).wait()
            return carry

        lax.fori_loop(0, n_tail, start_t, 0)
        lax.fori_loop(0, n_tail, wait_t, 0)

    for kk in range(TOP_K):
        pltpu.make_async_copy(hn_ref, xb_ref.at[pl.ds(0, tm)], sem).wait()


def _dispatch(hn, dest, counts, padded, pstart, n_rows, blk, tm=256):
    M, D = hn.shape
    zrows = blk // 2
    assert zrows & (zrows - 1) == 0 and n_rows % blk == 0
    kernel = functools.partial(_dispatch_kernel, tm=tm)
    grid_spec = pltpu.PrefetchScalarGridSpec(
        num_scalar_prefetch=3,
        grid=(M // tm,),
        in_specs=[pl.BlockSpec((1, 1, TOP_K * tm), lambda i, *_: (i, 0, 0)),
                  pl.BlockSpec((tm, D), lambda i, *_: (i, 0))],
        out_specs=pl.BlockSpec(memory_space=pl.ANY),
        scratch_shapes=[pltpu.SMEM((1, TOP_K * tm), jnp.int32), pltpu.VMEM((zrows, D), F32),
                        pltpu.SemaphoreType.DMA(()), pltpu.SemaphoreType.DMA(()), pltpu.SemaphoreType.DMA(())],
    )
    return pl.pallas_call(
        kernel,
        grid_spec=grid_spec,
        out_shape=jax.ShapeDtypeStruct((n_rows, D), F32),
        compiler_params=_cparams(("arbitrary",)),
        name="dispatch",
    )(counts, padded, pstart, dest.reshape(M // tm, 1, TOP_K * tm), hn)


def _expert_kernel(be_ref, nu_ref, x_ref, wu_ref, bu_ref, wd_ref, bdn_ref, y_ref, wu_s, wd_s, *, f, n_split):
    i = pl.program_id(0)
    e = be_ref[i]
    changed = jnp.logical_or(i == 0, e != be_ref[jnp.maximum(i - 1, 0)])

    @pl.when(changed)
    def _():
        wu_s[...] = wu_ref[0].astype(BF16)
        wd_s[...] = wd_ref[0].astype(BF16)

    @pl.when(i < nu_ref[0])
    def _():
        rows = x_ref.shape[0] // n_split
        hcats = [_dot(x_ref[j * rows:(j + 1) * rows, :].astype(BF16), wu_s[...]) + bu_ref[0] for j in range(n_split)]
        for j, hcat in enumerate(hcats):
            gate = jnp.minimum(hcat[:, :f], SWIGLU_LIMIT)
            lin = jnp.clip(hcat[:, f:], -SWIGLU_LIMIT, SWIGLU_LIMIT)
            act = (lin + 1.0) * (gate * jax.nn.sigmoid(SWIGLU_ALPHA * gate))
            y_ref[j * rows:(j + 1) * rows, :] = _dot(act.astype(BF16), wd_s[...]) + bdn_ref[0]

    @pl.when(i >= nu_ref[0])
    def _():
        y_ref[...] = jnp.zeros_like(y_ref)


def _experts(x_buf, blk_expert, n_used, w_up, b_up, w_down, b_down, blk, n_split=2):
    P, D = x_buf.shape
    E, _, f2 = w_up.shape
    f = f2 // 2
    kernel = functools.partial(_expert_kernel, f=f, n_split=n_split)
    grid_spec = pltpu.PrefetchScalarGridSpec(
        num_scalar_prefetch=2,
        grid=(P // blk,),
        in_specs=[
            pl.BlockSpec((blk, D), lambda i, be, nu: (jnp.minimum(i, nu[0] - 1), 0)),
            pl.BlockSpec((1, D, f2), lambda i, be, nu: (be[i], 0, 0)),
            pl.BlockSpec((1, 1, f2), lambda i, be, nu: (be[i], 0, 0)),
            pl.BlockSpec((1, f, D), lambda i, be, nu: (be[i], 0, 0)),
            pl.BlockSpec((1, 1, D), lambda i, be, nu: (be[i], 0, 0)),
        ],
        out_specs=pl.BlockSpec((blk, D), lambda i, be, nu: (i, 0)),
        scratch_shapes=[pltpu.VMEM((D, f2), BF16), pltpu.VMEM((f, D), BF16)],
    )
    return pl.pallas_call(
        kernel,
        grid_spec=grid_spec,
        out_shape=jax.ShapeDtypeStruct((P, D), F32),
        compiler_params=_cparams(("arbitrary",)),
        name="experts",
    )(blk_expert, n_used, x_buf, w_up, b_up.reshape(E, 1, f2), w_down, b_down.reshape(E, 1, D))


def _combine_kernel(dfirst_ref, dnext_ref, h_ref, g_ref, fw_ref, yb_ref, o_ref, dest_smem, yg_s, sem_idx, sem, *, tm):
    i = pl.program_id(0)
    n = pl.num_programs(0)

    def gather_tile(dref, slot):
        cp = pltpu.make_async_copy(dref.at[0], dest_smem, sem_idx)
        cp.start()
        cp.wait()

        def issue(t, carry):
            for kk in range(TOP_K):
                row = dest_smem[0, t * TOP_K + kk]
                pltpu.make_async_copy(yb_ref.at[pl.ds(row, 1)], yg_s.at[slot, kk, pl.ds(t, 1)], sem.at[slot]).start()
            return carry

        lax.fori_loop(0, tm, issue, 0)

    @pl.when(i == 0)
    def _():
        gather_tile(dfirst_ref, 0)

    @pl.when(i + 1 < n)
    def _():
        gather_tile(dnext_ref, (i + 1) % 2)

    slot = i % 2
    g = g_ref[...]
    acc = h_ref[...]
    for kk in range(TOP_K):
        pltpu.make_async_copy(yb_ref.at[pl.ds(0, tm)], yg_s.at[slot, kk], sem.at[slot]).wait()
    for kk in range(TOP_K):
        acc = acc + g[:, kk:kk + 1] * yg_s[slot, kk]
    ms = jnp.mean(acc * acc, axis=-1, keepdims=True)
    o_ref[...] = acc * lax.rsqrt(ms + RMS_EPS) * fw_ref[...]


def _combine(h, y_buf, dest, gates, final_w, tm=256):
    M, D = h.shape
    kernel = functools.partial(_combine_kernel, tm=tm)
    n = M // tm
    dest_tiles = dest.reshape(n, 1, TOP_K * tm)
    return pl.pallas_call(
        kernel,
        grid=(n,),
        in_specs=[
            pl.BlockSpec((1, 1, TOP_K * tm), lambda i: (0, 0, 0)),
            pl.BlockSpec((1, 1, TOP_K * tm), lambda i: (jnp.minimum(i + 1, n - 1), 0, 0)),
            pl.BlockSpec((tm, D), lambda i: (i, 0)),
            pl.BlockSpec((tm, LANES), lambda i: (i, 0)),
            pl.BlockSpec((1, D), lambda i: (0, 0)),
            pl.BlockSpec(memory_space=pl.ANY),
        ],
        out_specs=pl.BlockSpec((tm, D), lambda i: (i, 0)),
        out_shape=jax.ShapeDtypeStruct((M, D), F32),
        scratch_shapes=[pltpu.SMEM((1, TOP_K * tm), jnp.int32), pltpu.VMEM((2, TOP_K, tm, D), F32),
                        pltpu.SemaphoreType.DMA(()), pltpu.SemaphoreType.DMA((2,))],
        compiler_params=_cparams(("arbitrary",)),
        name="combine",
    )(dest_tiles, dest_tiles, h, gates, final_w.reshape(1, D).astype(F32), y_buf)


def kernel(x, norm1_w, w_in, attn_rpb, attn_norm_w, rwkv_mu_prev, rwkv_mu_next, rwkv_w0, rwkv_w2, rwkv_a0, rwkv_a2, rwkv_g2, rwkv_k_k, rwkv_k_a, rwkv_r_k, rwkv_ln_w, rwkv_ln_b, w_o, norm2_w, router_w, router_b, expert_w_up, expert_b_up, expert_w_down, expert_b_down, final_norm_w):
    B, T, D = x.shape
    M = B * T
    depth = norm1_w.shape[0]
    assert depth == 1, "the combine stage folds in the final norm, so it must follow the only layer"
    c = rwkv_k_k.shape[-1]
    att_w = attn_norm_w.shape[-1]
    blk = 512
    h = x.reshape(M, D)
    for l in range(depth):
        qkv, p = _in_proj(h, norm1_w[l], w_in[l].astype(BF16), 3 * att_w)
        att = _na2d(qkv.reshape(B, T, 3 * att_w), _na_bias_table(attn_rpb[l]), att_w)
        kq, rq, kd, bk, egl, v, bonus, gate = _rwkv_prep(
            p.reshape(B, T, -1), rwkv_mu_prev[l], rwkv_mu_next[l], rwkv_w0[l].reshape(-1), rwkv_w2[l],
            rwkv_a0[l].reshape(-1), rwkv_a2[l], rwkv_g2[l], rwkv_k_k[l], rwkv_k_a[l], rwkv_r_k[l].reshape(-1))
        y_fwd, y_bwd = _rwkv_scan(kq, rq, kd, bk, v, egl)
        h, hn, idx_pad, gates_pad, rank_pad, counts = _mix(
            h, att.reshape(M, att_w), y_fwd.reshape(M, c), y_bwd.reshape(M, c), bonus.reshape(M, c),
            gate.reshape(M, c),
            attn_norm_w[l], rwkv_ln_w[l], rwkv_ln_b[l], w_o[l], norm2_w[l], router_w[l], router_b[l])
        dest, cnt, padded, pstart, blk_expert, n_used = _route(
            idx_pad[:, :TOP_K], rank_pad[:, :TOP_K], counts[0, :N_EXPERTS], blk)
        n_rows = (M * TOP_K + N_EXPERTS * blk) // blk * blk
        x_buf = _dispatch(hn, dest, cnt, padded, pstart, n_rows, blk)
        y_buf = _experts(x_buf, blk_expert, n_used, expert_w_up[l], expert_b_up[l], expert_w_down[l],
                         expert_b_down[l], blk)
        h = _combine(h, y_buf, dest, gates_pad, final_norm_w)
    return h.reshape(B, T, D)
```

```python
import functools

import numpy as np
import jax
import jax.numpy as jnp
from jax import lax
from jax.experimental import pallas as pl
from jax.experimental.pallas import tpu as pltpu

F32 = jnp.float32
BF16 = jnp.bfloat16

HEAD_DIM = 64
GRID_W = 64
WIN_H = 8
WIN_W = 16
N_EXPERTS = 32
TOP_K = 4
SWIGLU_ALPHA = 1.702
SWIGLU_LIMIT = 7.0
RMS_EPS = 1e-5
GN_EPS = 64e-5
D_LORA = 64
D_GATE_LORA = 128
CHUNK = 64
LANES = 128
SUBLANES = 8
MASK_NEG = -1e30
VMEM_LIMIT = 56 * 1024 * 1024


def _cparams(sem):
    return pltpu.CompilerParams(dimension_semantics=sem, vmem_limit_bytes=VMEM_LIMIT)


def _dot(a, b):
    return jnp.dot(a, b, preferred_element_type=F32)


def _split2(x):
    hi = x.astype(BF16)
    lo = (x - hi.astype(F32)).astype(BF16)
    return hi, lo


def _dot_exact_lhs(a01, x):
    hi, lo = _split2(x)
    return _dot(a01, hi) + _dot(a01, lo)


def _dot_exact_rhs(x, b01):
    hi, lo = _split2(x)
    return _dot(hi, b01) + _dot(lo, b01)


def _inproj_kernel(x_ref, nw_ref, w_ref, qkv_ref, p_ref):
    x = x_ref[...]
    ms = jnp.mean(x * x, axis=-1, keepdims=True)
    hn = (x * lax.rsqrt(ms + RMS_EPS) * nw_ref[...]).astype(BF16)
    n_qkv = qkv_ref.shape[-1]
    qkv_ref[...] = _dot(hn, w_ref[:, :n_qkv]).astype(BF16)
    p_ref[...] = _dot(hn, w_ref[:, n_qkv:]).astype(BF16)


def _in_proj(x2d, norm_w, w_in_bf16, n_qkv, tm=512):
    M, D = x2d.shape
    n_all = w_in_bf16.shape[1]
    n_p = n_all - n_qkv
    return pl.pallas_call(
        _inproj_kernel,
        grid=(M // tm,),
        in_specs=[
            pl.BlockSpec((tm, D), lambda i: (i, 0)),
            pl.BlockSpec((1, D), lambda i: (0, 0)),
            pl.BlockSpec((D, n_all), lambda i: (0, 0)),
        ],
        out_specs=[
            pl.BlockSpec((tm, n_qkv), lambda i: (i, 0)),
            pl.BlockSpec((tm, n_p), lambda i: (i, 0)),
        ],
        out_shape=[jax.ShapeDtypeStruct((M, n_qkv), BF16), jax.ShapeDtypeStruct((M, n_p), BF16)],
        compiler_params=_cparams(("parallel",)),
        name="in_proj",
    )(x2d, norm_w.reshape(1, D), w_in_bf16)


def _na_bias_table(rpb):
    H = rpb.shape[0]
    cols = np.arange(GRID_W)
    col_start = np.clip(cols - WIN_W // 2, 0, GRID_W - WIN_W)
    kc = np.arange(GRID_W)
    valid = (kc[None, :] >= col_start[:, None]) & (kc[None, :] < col_start[:, None] + WIN_W)
    dc = kc[None, :] - cols[:, None] + (WIN_W - 1)
    var = np.arange(WIN_H)
    wi = np.arange(WIN_H)
    dr = wi[None, :] - var[:, None] + (WIN_H - 1)
    row_sel = (dr[:, :, None] == np.arange(2 * WIN_H - 1)).astype(np.float32)
    col_sel = ((dc[:, :, None] == np.arange(2 * WIN_W - 1)) & valid[:, :, None]).astype(np.float32)
    tab = jnp.einsum("vir,hrd,xyd->vhxiy", row_sel, rpb.astype(F32), col_sel, precision=lax.Precision.HIGHEST)
    tab = jnp.where(valid[None, None, :, None, :], tab, MASK_NEG)
    return tab.reshape(WIN_H, H, GRID_W, WIN_H * GRID_W)


def _na_kernel(q_ref, k_ref, v_ref, bias_ref, o_ref, *, rows_per_step, n_rows):
    i = pl.program_id(2)
    lane = lax.broadcasted_iota(jnp.int32, (1, LANES), 1)
    head0 = lane < HEAD_DIM
    scale = HEAD_DIM ** -0.5
    nkeys = WIN_H * GRID_W

    def scores(rr):
        r = i * rows_per_step + rr
        r_start = jnp.clip(r - WIN_H // 2, 0, n_rows - WIN_H)
        var = r - r_start
        q = q_ref[0, rr * GRID_W:(rr + 1) * GRID_W, :]
        koff = pl.multiple_of(r_start * GRID_W, GRID_W)
        kwin = k_ref[0, pl.ds(koff, nkeys), :]
        zero = jnp.zeros_like(q)
        q2 = jnp.concatenate([jnp.where(head0, q, zero), jnp.where(head0, zero, q)], axis=0)
        s = lax.dot_general(q2, kwin, (((1,), (1,)), ((), ())), preferred_element_type=F32)
        return koff, s * scale + bias_ref[var].reshape(2 * GRID_W, nkeys)

    def finish(rr, koff, s):
        vwin = v_ref[0, pl.ds(koff, nkeys), :]
        m = jnp.max(s, axis=-1, keepdims=True)
        e = jnp.exp(s - m)
        denom = jnp.sum(e, axis=-1, keepdims=True)
        o = _dot(e.astype(BF16), vwin) / denom
        o_ref[0, rr * GRID_W:(rr + 1) * GRID_W, :] = jnp.where(head0, o[:GRID_W], o[GRID_W:]).astype(o_ref.dtype)

    ahead = 3
    pending = [scores(rr) for rr in range(min(ahead, rows_per_step))]
    for rr in range(rows_per_step):
        if rr + ahead < rows_per_step:
            pending.append(scores(rr + ahead))
        finish(rr, *pending.pop(0))


def _na2d(qkv, bias_tab, att_w, rows_per_step=16):
    B, T, _ = qkv.shape
    n_rows = T // GRID_W
    assert n_rows >= WIN_H and n_rows % rows_per_step == 0
    n_pairs = att_w // LANES
    tq = rows_per_step * GRID_W
    kernel = functools.partial(_na_kernel, rows_per_step=rows_per_step, n_rows=n_rows)
    return pl.pallas_call(
        kernel,
        grid=(B, n_pairs, n_rows // rows_per_step),
        in_specs=[
            pl.BlockSpec((1, tq, LANES), lambda b, hp, i: (b, i, hp)),
            pl.BlockSpec((1, T, LANES), lambda b, hp, i: (b, 0, n_pairs + hp)),
            pl.BlockSpec((1, T, LANES), lambda b, hp, i: (b, 0, 2 * n_pairs + hp)),
            pl.BlockSpec((WIN_H, 2, GRID_W, WIN_H * GRID_W), lambda b, hp, i: (0, hp, 0, 0)),
        ],
        out_specs=pl.BlockSpec((1, tq, LANES), lambda b, hp, i: (b, i, hp)),
        out_shape=jax.ShapeDtypeStruct((B, T, att_w), BF16),
        compiler_params=_cparams(("parallel", "parallel", "arbitrary")),
        name="na2d",
    )(qkv, qkv, qkv, bias_tab)


def _prep_kernel(p_ref, pprev_ref, pnext_ref, mup_ref, mun_ref, w0_ref, w2_ref, a0_ref, a2_ref, g2_ref,
                 kk_ref, ka_ref, rk_ref, bd_ref, tri_ref,
                 kq_ref, rq_ref, kd_ref, bk_ref, egl_ref, v_ref, bonus_ref, gate_ref, *, tb, c):
    i = pl.program_id(1)
    nb = pl.num_programs(1)
    p = p_ref[0].astype(F32)
    halo = pprev_ref.shape[1]
    prow = jnp.where(i > 0, pprev_ref[0, halo - 1:halo, :].astype(F32), 0.0)
    nrow = jnp.where(i < nb - 1, pnext_ref[0, 0:1, :].astype(F32), 0.0)
    ridx = lax.broadcasted_iota(jnp.int32, (tb, 1), 0)
    p_prev = jnp.where(ridx == 0, prow, pltpu.roll(p, 1, 0))
    p_next = jnp.where(ridx == tb - 1, nrow, pltpu.roll(p, tb - 1, 0))
    pm = p + mup_ref[...] * (p_prev - p) + mun_ref[...] * (p_next - p)
    r = pm[:, :c]
    k = pm[:, c:2 * c]
    v = pm[:, 2 * c:3 * c]
    o = 3 * c
    dw = pm[:, o:o + 2 * D_LORA]
    da = pm[:, o + 2 * D_LORA:o + 4 * D_LORA]
    dg = pm[:, o + 4 * D_LORA:o + 4 * D_LORA + D_GATE_LORA]

    wl = w0_ref[...] + _dot(jnp.tanh(dw).astype(BF16), w2_ref[...])
    neg = -wl
    softplus = jnp.maximum(neg, 0.0) + jnp.log(1.0 + jnp.exp(-jnp.abs(neg)))
    logw = -jnp.exp(-softplus - 0.5)
    a = jax.nn.sigmoid(a0_ref[...] + _dot(da.astype(BF16), a2_ref[...]))
    gate = _dot(jax.nn.sigmoid(dg).astype(BF16), g2_ref[...])

    bd = bd_ref[...]
    kkr = k * kk_ref[...]
    ssq = _dot_exact_rhs(kkr * kkr, bd)
    kk = kkr / jnp.maximum(jnp.sqrt(ssq), 1e-12)

    ksum = jnp.zeros_like(k)
    nchunk = tb // CHUNK
    for z in range(2):
        a_z = a[:, z * c:(z + 1) * c]
        lw = logw[:, z * c:(z + 1) * c]
        kdir = k * (1.0 + (a_z - 1.0) * ka_ref[...])
        ksum = ksum + kdir
        b_z = a_z * kk
        g_incl = _dot_exact_lhs(tri_ref[z], lw)
        last = CHUNK - 1 if z == 0 else 0
        g_all = g_incl.reshape(nchunk, CHUNK, c)[:, last:last + 1, :]
        e_neg = jnp.exp(-g_incl)
        kq_ref[z, 0] = (kk * jnp.exp(g_incl - lw)).astype(BF16)
        rq_ref[z, 0] = (r * jnp.exp(g_incl)).astype(BF16)
        kd_ref[z, 0] = (kdir * e_neg).astype(BF16)
        bk_ref[z, 0] = (b_z * e_neg).astype(BF16)
        egl_ref[z, 0] = jnp.exp(g_all)
    coef = _dot_exact_rhs(r * ksum * rk_ref[...], bd)
    v_ref[0] = v.astype(BF16)
    bonus_ref[0] = (coef * v).astype(BF16)
    gate_ref[0] = gate.astype(BF16)


def _rwkv_prep(p, mu_prev, mu_next, w0, w2, a0, a2, g2, k_k, k_a, r_k, tb=256):
    B, T, pw = p.shape
    c = k_k.shape[0]
    assert T % tb == 0 and tb % CHUNK == 0
    halo = 16
    nhb = T // halo
    nchunk = tb // CHUNK
    w2bd = jnp.zeros((2 * D_LORA, 2 * c), F32)
    a2bd = jnp.zeros((2 * D_LORA, 2 * c), F32)
    for z in range(2):
        w2bd = w2bd.at[z * D_LORA:(z + 1) * D_LORA, z * c:(z + 1) * c].set(w2[z])
        a2bd = a2bd.at[z * D_LORA:(z + 1) * D_LORA, z * c:(z + 1) * c].set(a2[z])
    ch = np.arange(c) // HEAD_DIM
    bd = jnp.asarray((ch[:, None] == ch[None, :]).astype(np.float32), BF16)
    t = np.arange(tb)
    same = (t[:, None] // CHUNK) == (t[None, :] // CHUNK)
    tri = np.stack([same & (t[None, :] <= t[:, None]), same & (t[None, :] >= t[:, None])]).astype(np.float32)
    tri = jnp.asarray(tri, BF16)
    row = lambda u: u.reshape(1, -1).astype(F32)
    const = lambda shape: pl.BlockSpec(shape, lambda b, i: (0,) * len(shape))
    dir_out = pl.BlockSpec((2, 1, tb, c), lambda b, i: (0, b, i, 0))
    tok_out = pl.BlockSpec((1, tb, c), lambda b, i: (b, i, 0))
    kernel = functools.partial(_prep_kernel, tb=tb, c=c)
    return pl.pallas_call(
        kernel,
        grid=(B, T // tb),
        in_specs=[
            pl.BlockSpec((1, tb, pw), lambda b, i: (b, i, 0)),
            pl.BlockSpec((1, halo, pw), lambda b, i: (b, jnp.maximum(i * (tb // halo) - 1, 0), 0)),
            pl.BlockSpec((1, halo, pw), lambda b, i: (b, jnp.minimum((i + 1) * (tb // halo), nhb - 1), 0)),
            const((1, pw)), const((1, pw)),
            const((1, 2 * c)), const((2 * D_LORA, 2 * c)),
            const((1, 2 * c)), const((2 * D_LORA, 2 * c)),
            const((D_GATE_LORA, c)),
            const((1, c)), const((1, c)), const((1, c)),
            const((c, c)), const((2, tb, tb)),
        ],
        out_specs=[dir_out, dir_out, dir_out, dir_out,
                   pl.BlockSpec((2, 1, nchunk, 1, c), lambda b, i: (0, b, i, 0, 0)),
                   tok_out, tok_out, tok_out],
        out_shape=[jax.ShapeDtypeStruct((2, B, T, c), BF16)] * 4
        + [jax.ShapeDtypeStruct((2, B, T // CHUNK, 1, c), F32)]
        + [jax.ShapeDtypeStruct((B, T, c), BF16)] * 3,
        compiler_params=_cparams(("parallel", "parallel")),
        name="rwkv_prep",
    )(p, p, p, row(mu_prev), row(mu_next), row(w0), w2bd.astype(BF16), row(a0), a2bd.astype(BF16),
      g2.astype(BF16), row(k_k), row(k_a), row(r_k), bd, tri)


def _bmm(a, b):
    return jnp.einsum("cij,cjk->cik", a.astype(BF16), b.astype(BF16), preferred_element_type=F32)


def _bmm_nt(a, b):
    return jnp.einsum("cik,cjk->cij", a.astype(BF16), b.astype(BF16), preferred_element_type=F32)


def _scan_kernel(*refs, cg, n_pairs):
    in_refs = (refs[:6], refs[6:12])
    y_refs = refs[12:14]
    h_s, mc_s, cc_s, hs_s = refs[14:]
    gi = pl.program_id(2)

    @pl.when(gi == 0)
    def _():
        h_s[...] = jnp.zeros_like(h_s)

    two = 2 * CHUNK
    lane = lax.broadcasted_iota(jnp.int32, (1, 1, LANES), 2)
    head0 = lane < HEAD_DIM
    row = lax.broadcasted_iota(jnp.int32, (1, two, two), 1)
    col = lax.broadcasted_iota(jnp.int32, (1, two, two), 2)
    same = (row // CHUNK) == (col // CHUNK)
    ahead = (row % CHUNK) - (col % CHUNK)
    eye = (row == col).astype(F32)
    chains = [(z, pp) for z in range(2) for pp in range(n_pairs)]

    def stacked(ref, pp):
        x = ref[0, :, pp * LANES:(pp + 1) * LANES].reshape(cg, CHUNK, LANES)
        zero = jnp.zeros_like(x)
        return jnp.concatenate([jnp.where(head0, x, zero), jnp.where(head0, zero, x)], axis=1)

    ops = []
    for z, pp in chains:
        kq_ref, rq_ref, kd_ref, bk_ref, v_ref, egl_ref = in_refs[z]
        ops.append(dict(
            kq=stacked(kq_ref.at[0], pp), rq=stacked(rq_ref.at[0], pp), kd=stacked(kd_ref.at[0], pp),
            bk=stacked(bk_ref.at[0], pp), v=stacked(v_ref, pp),
            egl=egl_ref[0, 0, :, :, pp * LANES:(pp + 1) * LANES],
            strict=same & ((ahead > 0) if z == 0 else (ahead < 0)),
            incl=same & ((ahead >= 0) if z == 0 else (ahead <= 0))))

    for o in ops:
        gram = _bmm_nt(jnp.concatenate([o["kq"], o["rq"]], axis=1), jnp.concatenate([o["kd"], o["bk"]], axis=1))
        o["a_kk"] = jnp.where(o["strict"], gram[:, :two, :two], 0.0)
        o["pw"] = -jnp.where(o["strict"], gram[:, :two, two:], 0.0)
        o["a_rk"] = jnp.where(o["incl"], gram[:, two:, :two], 0.0)
        o["a_rb"] = jnp.where(o["incl"], gram[:, two:, two:], 0.0)
        o["tinv"] = eye + o["pw"]

    for o in ops:
        o["pw"] = _bmm(o["pw"], o["pw"])
    for _ in range(4):
        for o in ops:
            both = _bmm(o["pw"], jnp.concatenate([o["pw"].astype(BF16), o["tinv"].astype(BF16)], axis=2))
            o["pw"] = both[:, :, :two]
            o["tinv"] = o["tinv"] + both[:, :, two:]
    for o in ops:
        o["tinv"] = o["tinv"] + _bmm(o["pw"], o["tinv"])
    for o in ops:
        o["akv"] = _bmm(o["a_kk"], o["v"])
    for o in ops:
        o["wu"] = _bmm(o["tinv"], jnp.concatenate([o["kq"], o["akv"].astype(BF16)], axis=2))
    for n, o in enumerate(ops):
        bkg_t = jnp.swapaxes(o["bk"].astype(F32) * o["egl"], 1, 2)
        kdg_t = jnp.swapaxes(o["kd"].astype(F32) * o["egl"], 1, 2)
        bwu = _bmm(bkg_t, o["wu"])
        mc_s[n] = (eye * o["egl"] - bwu[:, :, :LANES]).astype(BF16)
        cc_s[n] = _bmm(kdg_t, o["v"]) - bwu[:, :, LANES:]
    for o in ops:
        rwu = _bmm(o["a_rb"], o["wu"])
        o["q"] = o["rq"].astype(F32) - rwu[:, :, :LANES]
        o["y0"] = _bmm(o["a_rk"], o["v"]) - rwu[:, :, LANES:]

    def body(ci, carry):
        for n, (z, _) in enumerate(chains):
            cidx = ci if z == 0 else cg - 1 - ci
            hb = h_s[n].astype(BF16)
            hs_s[n, cidx] = hb
            h_s[n] = _dot(mc_s[n, cidx], hb) + cc_s[n, cidx]
        return carry

    lax.fori_loop(0, cg, body, 0)
    for n, (z, pp) in enumerate(chains):
        y = _bmm(ops[n]["q"], hs_s[n]) + ops[n]["y0"]
        y = (y[:, :CHUNK, :] + y[:, CHUNK:, :]).reshape(cg * CHUNK, LANES)
        y_refs[z][0, :, pp * LANES:(pp + 1) * LANES] = y.astype(y_refs[z].dtype)


def _rwkv_scan(kq, rq, kd, bk, v, egl, cg=8, n_pairs=4):
    _, B, T, c = kq.shape
    lw = n_pairs * LANES
    tl = cg * CHUNK
    ng = T // tl
    assert T % tl == 0 and c % lw == 0
    n_chain = 2 * n_pairs

    in_specs, args = [], []
    for z in range(2):
        tmap = (lambda g: g) if z == 0 else (lambda g: ng - 1 - g)
        dir_in = pl.BlockSpec((1, 1, tl, lw), lambda b, hp, g, z=z, tmap=tmap: (z, b, tmap(g), hp))
        in_specs += [dir_in] * 4
        in_specs.append(pl.BlockSpec((1, tl, lw), lambda b, hp, g, tmap=tmap: (b, tmap(g), hp)))
        in_specs.append(pl.BlockSpec((1, 1, cg, 1, lw), lambda b, hp, g, z=z, tmap=tmap: (z, b, tmap(g), 0, hp)))
        args += [kq, rq, kd, bk, v, egl]
    out_specs = [pl.BlockSpec((1, tl, lw), lambda b, hp, g: (b, g, hp)),
                 pl.BlockSpec((1, tl, lw), lambda b, hp, g: (b, ng - 1 - g, hp))]
    kernel = functools.partial(_scan_kernel, cg=cg, n_pairs=n_pairs)
    return pl.pallas_call(
        kernel,
        grid=(B, c // lw, ng),
        in_specs=in_specs,
        out_specs=out_specs,
        out_shape=[jax.ShapeDtypeStruct((B, T, c), BF16)] * 2,
        scratch_shapes=[pltpu.VMEM((n_chain, LANES, LANES), F32),
                        pltpu.VMEM((n_chain, cg, LANES, LANES), BF16),
                        pltpu.VMEM((n_chain, cg, LANES, LANES), F32),
                        pltpu.VMEM((n_chain, cg, LANES, LANES), BF16)],
        compiler_params=_cparams(("parallel", "parallel", "arbitrary")),
        name="rwkv_scan",
    )(*args)


def _mix_kernel(x_ref, att_ref, yf_ref, yb_ref, bonus_ref, gate_ref, anw_ref, lnw_ref, lnb_ref, bd_ref, wo_ref,
                n2w_ref, rw_ref, rb_ref, tril_ref, h_ref, hn_ref, idx_ref, gates_ref, rank_ref, cnt_ref, cnt_s):
    @pl.when(pl.program_id(0) == 0)
    def _():
        cnt_s[...] = jnp.zeros_like(cnt_s)

    bd = bd_ref[...]
    inv_n = 1.0 / HEAD_DIM
    att_w = att_ref.shape[-1]
    rows = tril_ref.shape[0]
    lane = lax.broadcasted_iota(jnp.int32, (rows, LANES), 1)
    cnt = cnt_s[...]
    n_sub = x_ref.shape[0] // rows
    subs = [slice(j * rows, (j + 1) * rows) for j in range(n_sub)]
    ys = [yf_ref[sl, :].astype(F32) + yb_ref[sl, :].astype(F32) for sl in subs]
    ycs = [y - _dot_exact_rhs(y, bd) * inv_n for y in ys]
    vrs = [_dot_exact_rhs(yc * yc, bd) * inv_n for yc in ycs]
    mixes = []
    for sl, yc, var in zip(subs, ycs, vrs):
        att = att_ref[sl, :].astype(F32)
        ms = jnp.mean(att * att, axis=-1, keepdims=True)
        att_n = att * lax.rsqrt(ms + RMS_EPS) * anw_ref[...]
        yn = yc * lax.rsqrt(var + GN_EPS) * lnw_ref[...] + lnb_ref[...]
        rk = (yn + bonus_ref[sl, :].astype(F32)) * gate_ref[sl, :].astype(F32)
        mixes.append(_dot(att_n.astype(BF16), wo_ref[:att_w, :]) + _dot(rk.astype(BF16), wo_ref[att_w:, :]))
    all_logits = []
    for sl, mix in zip(subs, mixes):
        h = x_ref[sl, :] + mix
        h_ref[sl, :] = h
        ms2 = jnp.mean(h * h, axis=-1, keepdims=True)
        hn = h * lax.rsqrt(ms2 + RMS_EPS) * n2w_ref[...]
        hn_ref[sl, :] = hn
        hn_hi, hn_lo = _split2(hn)
        all_logits.append(_dot(hn_hi, rw_ref[0]) + _dot(hn_lo, rw_ref[0]) + _dot(hn_hi, rw_ref[1]) + rb_ref[...])

    for j, logits in enumerate(all_logits):
        sl = slice(j * rows, (j + 1) * rows)
        cur = logits
        vals, idxs = [], []
        for _ in range(TOP_K):
            m = jnp.max(cur, axis=-1, keepdims=True)
            sel = jnp.min(jnp.where(cur == m, lane, LANES), axis=-1, keepdims=True)
            vals.append(m)
            idxs.append(sel)
            cur = jnp.where(lane == sel, -jnp.inf, cur)
        es = [jnp.exp(vk - vals[0]) for vk in vals]
        tot = es[0] + es[1] + es[2] + es[3]
        onehot = jnp.zeros(logits.shape, F32)
        for kk in range(TOP_K):
            onehot = onehot + (lane == idxs[kk]).astype(F32)
        before = cnt + _dot(tril_ref[...], onehot.astype(BF16))
        cnt = cnt + jnp.sum(onehot, axis=0, keepdims=True)

        idx_out = jnp.zeros(logits.shape, jnp.int32)
        g_out = jnp.zeros(logits.shape, F32)
        rank_out = jnp.zeros(logits.shape, F32)
        for kk in range(TOP_K):
            idx_out = jnp.where(lane == kk, idxs[kk], idx_out)
            g_out = jnp.where(lane == kk, es[kk] / tot, g_out)
            rank_k = jnp.sum(jnp.where(lane == idxs[kk], before, 0.0), axis=-1, keepdims=True)
            rank_out = jnp.where(lane == kk, rank_k, rank_out)
        idx_ref[sl, :] = idx_out
        gates_ref[sl, :] = g_out
        rank_ref[sl, :] = rank_out.astype(jnp.int32)
    cnt_s[...] = cnt
    cnt_ref[...] = cnt


def _mix(x2d, att, y_fwd, y_bwd, bonus, gate, attn_norm_w, ln_w, ln_b, w_o, norm2_w, router_w, router_b,
         tm=1024, n_split=4):
    M, D = x2d.shape
    c = y_fwd.shape[-1]
    att_w = att.shape[-1]
    ch = np.arange(c) // HEAD_DIM
    bd = jnp.asarray((ch[:, None] == ch[None, :]).astype(np.float32), BF16)
    t = np.arange(tm // n_split)
    tril = jnp.asarray((t[None, :] < t[:, None]).astype(np.float32), BF16)
    rw = jnp.zeros((D, LANES), F32).at[:, :N_EXPERTS].set(router_w)
    rw = jnp.stack(_split2(rw))
    rb = jnp.full((1, LANES), -jnp.inf, F32).at[0, :N_EXPERTS].set(router_b)
    row = lambda u: u.reshape(1, -1).astype(F32)
    const = lambda shape: pl.BlockSpec(shape, lambda i: (0,) * len(shape))
    tok = lambda w: pl.BlockSpec((tm, w), lambda i: (i, 0))
    return pl.pallas_call(
        _mix_kernel,
        grid=(M // tm,),
        in_specs=[
            tok(D), tok(att_w), tok(c), tok(c), tok(c), tok(c),
            const((1, att_w)), const((1, c)), const((1, c)), const((c, c)), const((att_w + c, D)),
            const((1, D)), const((2, D, LANES)), const((1, LANES)), const((tm // n_split, tm // n_split)),
        ],
        out_specs=[
            tok(D), tok(D),
            tok(LANES), tok(LANES), tok(LANES),
            const((1, LANES)),
        ],
        out_shape=[jax.ShapeDtypeStruct((M, D), F32), jax.ShapeDtypeStruct((M, D), F32),
                   jax.ShapeDtypeStruct((M, LANES), jnp.int32), jax.ShapeDtypeStruct((M, LANES), F32),
                   jax.ShapeDtypeStruct((M, LANES), jnp.int32), jax.ShapeDtypeStruct((1, LANES), F32)],
        scratch_shapes=[pltpu.VMEM((1, LANES), F32)],
        compiler_params=_cparams(("arbitrary",)),
        name="mix_router",
    )(x2d, att, y_fwd, y_bwd, bonus, gate, row(attn_norm_w), row(ln_w), row(ln_b), bd, w_o.astype(BF16),
      row(norm2_w), rw, rb, tril)


def _route(top_idx, rank, counts, blk):
    M = top_idx.shape[0]
    counts = counts.astype(jnp.int32)
    padded = (counts + blk - 1) // blk * blk
    pend = jnp.cumsum(padded)
    pstart = pend - padded
    experts = jnp.arange(N_EXPERTS, dtype=jnp.int32)
    first = jnp.sum(jnp.where(top_idx[:, :, None] == experts, pstart, 0), axis=-1)
    dest = (first + rank).astype(jnp.int32)
    n_blk = (M * TOP_K + N_EXPERTS * blk) // blk
    starts = jnp.arange(n_blk, dtype=jnp.int32) * blk
    blk_expert = jnp.sum((pend[None, :] <= starts[:, None]).astype(jnp.int32), axis=1)
    blk_expert = jnp.minimum(blk_expert, N_EXPERTS - 1).astype(jnp.int32)
    n_used = (pend[-1] // blk).reshape(1).astype(jnp.int32)
    return dest, counts, padded.astype(jnp.int32), pstart.astype(jnp.int32), blk_expert, n_used


def _dispatch_kernel(cnt_ref, pad_ref, first_ref, dest_ref, hn_ref, xb_ref, dest_smem, zero_s, sem_idx, sem, sem_pad,
                     *, tm):
    i = pl.program_id(0)
    cp = pltpu.make_async_copy(dest_ref.at[0], dest_smem, sem_idx)
    cp.start()
    cp.wait()

    def issue(t, carry):
        for kk in range(TOP_K):
            row = dest_smem[0, t * TOP_K + kk]
            pltpu.make_async_copy(hn_ref.at[pl.ds(t, 1)], xb_ref.at[pl.ds(row, 1)], sem).start()
        return carry

    lax.fori_loop(0, tm, issue, 0)

    @pl.when(i == 0)
    def _():
        zero_s[...] = jnp.zeros_like(zero_s)
        zrows = zero_s.shape[0]

        def zero_copy(pos, size):
            return pltpu.make_async_copy(zero_s.at[pl.ds(0, size)], xb_ref.at[pl.ds(pos, size)], sem_pad)

        def padding(e, act):
            pos = first_ref[e] + cnt_ref[e]
            length = pad_ref[e] - cnt_ref[e]
            end = first_ref[e] + pad_ref[e]
            size = zrows
            while size >= SUBLANES:
                has = (length & size) != 0
                end = end - jnp.where(has, size, 0)
                pl.when(has)(functools.partial(act, pl.multiple_of(end, SUBLANES), size))
                size //= 2
            for r in range(SUBLANES - 1):
                pl.when(r < (length & (SUBLANES - 1)))(functools.partial(act, pos + r, 1))

        def start_e(e, carry):
            padding(e, lambda pos, size: zero_copy(pos, size).start())
            return carry

        def wait_e(e, carry):
            padding(e, lambda pos, size: zero_copy(pos, size).wait())
            return carry

        lax.fori_loop(0, N_EXPERTS, start_e, 0)
        lax.fori_loop(0, N_EXPERTS, wait_e, 0)
        tail = first_ref[N_EXPERTS - 1] + pad_ref[N_EXPERTS - 1]
        n_tail = (xb_ref.shape[0] - tail) // zrows

        def start_t(j, carry):
            zero_copy(pl.multiple_of(tail + j * zrows, SUBLANES), zrows).start()
            return carry

        def wait_t(j, carry):
            zero_copy(pl.multiple_of(tail + j * zrows, SUBLANES), zrows).wait()
            return carry

        lax.fori_loop(0, n_tail, start_t, 0)
        lax.fori_loop(0, n_tail, wait_t, 0)

    for kk in range(TOP_K):
        pltpu.make_async_copy(hn_ref, xb_ref.at[pl.ds(0, tm)], sem).wait()


def _dispatch(hn, dest, counts, padded, pstart, n_rows, blk, tm=512):
    M, D = hn.shape
    zrows = blk // 2
    assert zrows & (zrows - 1) == 0 and n_rows % blk == 0
    kernel = functools.partial(_dispatch_kernel, tm=tm)
    grid_spec = pltpu.PrefetchScalarGridSpec(
        num_scalar_prefetch=3,
        grid=(M // tm,),
        in_specs=[pl.BlockSpec((1, 1, TOP_K * tm), lambda i, *_: (i, 0, 0)),
                  pl.BlockSpec((tm, D), lambda i, *_: (i, 0))],
        out_specs=pl.BlockSpec(memory_space=pl.ANY),
        scratch_shapes=[pltpu.SMEM((1, TOP_K * tm), jnp.int32), pltpu.VMEM((zrows, D), F32),
                        pltpu.SemaphoreType.DMA(()), pltpu.SemaphoreType.DMA(()), pltpu.SemaphoreType.DMA(())],
    )
    return pl.pallas_call(
        kernel,
        grid_spec=grid_spec,
        out_shape=jax.ShapeDtypeStruct((n_rows, D), F32),
        compiler_params=_cparams(("arbitrary",)),
        name="dispatch",
    )(counts, padded, pstart, dest.reshape(M // tm, 1, TOP_K * tm), hn)


def _expert_kernel(be_ref, nu_ref, x_ref, wu_ref, bu_ref, wd_ref, bdn_ref, y_ref, wu_s, wd_s, *, f, n_split):
    i = pl.program_id(0)
    e = be_ref[i]
    changed = jnp.logical_or(i == 0, e != be_ref[jnp.maximum(i - 1, 0)])

    @pl.when(changed)
    def _():
        wu_s[...] = wu_ref[0].astype(BF16)
        wd_s[...] = wd_ref[0].astype(BF16)

    @pl.when(i < nu_ref[0])
    def _():
        rows = x_ref.shape[0] // n_split
        hcats = [_dot(x_ref[j * rows:(j + 1) * rows, :].astype(BF16), wu_s[...]) + bu_ref[0] for j in range(n_split)]
        for j, hcat in enumerate(hcats):
            gate = jnp.minimum(hcat[:, :f], SWIGLU_LIMIT)
            lin = jnp.clip(hcat[:, f:], -SWIGLU_LIMIT, SWIGLU_LIMIT)
            act = (lin + 1.0) * (gate * jax.nn.sigmoid(SWIGLU_ALPHA * gate))
            y_ref[j * rows:(j + 1) * rows, :] = _dot(act.astype(BF16), wd_s[...]) + bdn_ref[0]

    @pl.when(i >= nu_ref[0])
    def _():
        y_ref[...] = jnp.zeros_like(y_ref)


def _experts(x_buf, blk_expert, n_used, w_up, b_up, w_down, b_down, blk, n_split=2):
    P, D = x_buf.shape
    E, _, f2 = w_up.shape
    f = f2 // 2
    kernel = functools.partial(_expert_kernel, f=f, n_split=n_split)
    grid_spec = pltpu.PrefetchScalarGridSpec(
        num_scalar_prefetch=2,
        grid=(P // blk,),
        in_specs=[
            pl.BlockSpec((blk, D), lambda i, be, nu: (jnp.minimum(i, nu[0] - 1), 0)),
            pl.BlockSpec((1, D, f2), lambda i, be, nu: (be[i], 0, 0)),
            pl.BlockSpec((1, 1, f2), lambda i, be, nu: (be[i], 0, 0)),
            pl.BlockSpec((1, f, D), lambda i, be, nu: (be[i], 0, 0)),
            pl.BlockSpec((1, 1, D), lambda i, be, nu: (be[i], 0, 0)),
        ],
        out_specs=pl.BlockSpec((blk, D), lambda i, be, nu: (i, 0)),
        scratch_shapes=[pltpu.VMEM((D, f2), BF16), pltpu.VMEM((f, D), BF16)],
    )
    return pl.pallas_call(
        kernel,
        grid_spec=grid_spec,
        out_shape=jax.ShapeDtypeStruct((P, D), F32),
        compiler_params=_cparams(("arbitrary",)),
        name="experts",
    )(blk_expert, n_used, x_buf, w_up, b_up.reshape(E, 1, f2), w_down, b_down.reshape(E, 1, D))


def _combine_kernel(dfirst_ref, dnext_ref, h_ref, g_ref, fw_ref, yb_ref, o_ref, dest_smem, yg_s, sem_idx, sem, *, tm):
    i = pl.program_id(0)
    n = pl.num_programs(0)

    def gather_tile(dref, slot):
        cp = pltpu.make_async_copy(dref.at[0], dest_smem, sem_idx)
        cp.start()
        cp.wait()

        def issue(t, carry):
            for kk in range(TOP_K):
                row = dest_smem[0, t * TOP_K + kk]
                pltpu.make_async_copy(yb_ref.at[pl.ds(row, 1)], yg_s.at[slot, kk, pl.ds(t, 1)], sem.at[slot]).start()
            return carry

        lax.fori_loop(0, tm, issue, 0)

    @pl.when(i == 0)
    def _():
        gather_tile(dfirst_ref, 0)

    @pl.when(i + 1 < n)
    def _():
        gather_tile(dnext_ref, (i + 1) % 2)

    slot = i % 2
    g = g_ref[...]
    acc = h_ref[...]
    for kk in range(TOP_K):
        pltpu.make_async_copy(yb_ref.at[pl.ds(0, tm)], yg_s.at[slot, kk], sem.at[slot]).wait()
    for kk in range(TOP_K):
        acc = acc + g[:, kk:kk + 1] * yg_s[slot, kk]
    ms = jnp.mean(acc * acc, axis=-1, keepdims=True)
    o_ref[...] = acc * lax.rsqrt(ms + RMS_EPS) * fw_ref[...]


def _combine(h, y_buf, dest, gates, final_w, tm=512):
    M, D = h.shape
    kernel = functools.partial(_combine_kernel, tm=tm)
    n = M // tm
    dest_tiles = dest.reshape(n, 1, TOP_K * tm)
    return pl.pallas_call(
        kernel,
        grid=(n,),
        in_specs=[
            pl.BlockSpec((1, 1, TOP_K * tm), lambda i: (0, 0, 0)),
            pl.BlockSpec((1, 1, TOP_K * tm), lambda i: (jnp.minimum(i + 1, n - 1), 0, 0)),
            pl.BlockSpec((tm, D), lambda i: (i, 0)),
            pl.BlockSpec((tm, LANES), lambda i: (i, 0)),
            pl.BlockSpec((1, D), lambda i: (0, 0)),
            pl.BlockSpec(memory_space=pl.ANY),
        ],
        out_specs=pl.BlockSpec((tm, D), lambda i: (i, 0)),
        out_shape=jax.ShapeDtypeStruct((M, D), F32),
        scratch_shapes=[pltpu.SMEM((1, TOP_K * tm), jnp.int32), pltpu.VMEM((2, TOP_K, tm, D), F32),
                        pltpu.SemaphoreType.DMA(()), pltpu.SemaphoreType.DMA((2,))],
        compiler_params=_cparams(("arbitrary",)),
        name="combine",
    )(dest_tiles, dest_tiles, h, gates, final_w.reshape(1, D).astype(F32), y_buf)


def kernel(x, norm1_w, w_in, attn_rpb, attn_norm_w, rwkv_mu_prev, rwkv_mu_next, rwkv_w0, rwkv_w2, rwkv_a0, rwkv_a2, rwkv_g2, rwkv_k_k, rwkv_k_a, rwkv_r_k, rwkv_ln_w, rwkv_ln_b, w_o, norm2_w, router_w, router_b, expert_w_up, expert_b_up, expert_w_down, expert_b_down, final_norm_w):
    B, T, D = x.shape
    M = B * T
    depth = norm1_w.shape[0]
    assert depth == 1, "the combine stage folds in the final norm, so it must follow the only layer"
    c = rwkv_k_k.shape[-1]
    att_w = attn_norm_w.shape[-1]
    blk = 512
    h = x.reshape(M, D)
    for l in range(depth):
        qkv, p = _in_proj(h, norm1_w[l], w_in[l].astype(BF16), 3 * att_w)
        att = _na2d(qkv.reshape(B, T, 3 * att_w), _na_bias_table(attn_rpb[l]), att_w)
        kq, rq, kd, bk, egl, v, bonus, gate = _rwkv_prep(
            p.reshape(B, T, -1), rwkv_mu_prev[l], rwkv_mu_next[l], rwkv_w0[l].reshape(-1), rwkv_w2[l],
            rwkv_a0[l].reshape(-1), rwkv_a2[l], rwkv_g2[l], rwkv_k_k[l], rwkv_k_a[l], rwkv_r_k[l].reshape(-1))
        y_fwd, y_bwd = _rwkv_scan(kq, rq, kd, bk, v, egl)
        h, hn, idx_pad, gates_pad, rank_pad, counts = _mix(
            h, att.reshape(M, att_w), y_fwd.reshape(M, c), y_bwd.reshape(M, c), bonus.reshape(M, c),
            gate.reshape(M, c),
            attn_norm_w[l], rwkv_ln_w[l], rwkv_ln_b[l], w_o[l], norm2_w[l], router_w[l], router_b[l])
        dest, cnt, padded, pstart, blk_expert, n_used = _route(
            idx_pad[:, :TOP_K], rank_pad[:, :TOP_K], counts[0, :N_EXPERTS], blk)
        n_rows = (M * TOP_K + N_EXPERTS * blk) // blk * blk
        x_buf = _dispatch(hn, dest, cnt, padded, pstart, n_rows, blk)
        y_buf = _experts(x_buf, blk_expert, n_used, expert_w_up[l], expert_b_up[l], expert_w_down[l],
                         expert_b_down[l], blk)
        h = _combine(h, y_buf, dest, gates_pad, final_norm_w)
    return h.reshape(B, T, D)
```
